```python
import math
import jax
import jax.numpy as jnp
from jax import lax
import numpy as np

D_MODEL = 2048
BATCH = 4
SEQ = 4096
DEPTH = 1

D_MIX = D_MODEL
D_SSM = D_MIX // 2
D_ATTN = D_MIX - D_SSM
SSM_HEAD_DIM = 64
SSM_HEADS = D_SSM // SSM_HEAD_DIM
SSM_GROUPS = 2
SSM_HPG = SSM_HEADS // SSM_GROUPS
SSM_STATE = 128
CONV_WIDTH = 4
CONV_CH = D_SSM + 2 * SSM_GROUPS * SSM_STATE
CHUNK = 128
DT_MIN = 0.001
DT_MAX = 0.1
DIFF_HEAD_DIM = 64
DIFF_V_DIM = 2 * DIFF_HEAD_DIM
DIFF_HEADS = D_ATTN // DIFF_V_DIM
D_QK = DIFF_HEADS * 2 * DIFF_HEAD_DIM
D_V = DIFF_HEADS * DIFF_V_DIM
Q_BLOCK = 128
ROPE_THETA = 10000.0
N_EXPERTS = 32
TOP_K = 4
D_EXPERT = D_MODEL
SWIGLU_ALPHA = 1.702
SWIGLU_LIMIT = 7.0
NORM_EPS = 1e-5

IN_WIDTHS = (D_SSM, CONV_CH, SSM_HEADS, D_QK, D_QK, D_V)
IN_COLS = sum(IN_WIDTHS)
IN_OFFSETS = tuple(int(o) for o in np.cumsum(IN_WIDTHS)[:-1])

kernel_name = "hymba_ssd_diffattn_moe_block"


def rms_norm(x, g, eps=NORM_EPS):
    xf = x.astype(jnp.float32)
    y = xf * lax.rsqrt(jnp.mean(xf * xf, axis=-1, keepdims=True) + eps)
    return (y * g.astype(jnp.float32)).astype(x.dtype)


def causal_depthwise_conv(u, w, b):
    ch = u.shape[-1]
    out = lax.conv_general_dilated(
        u, w[:, None, :].astype(u.dtype), window_strides=(1,),
        padding=[(CONV_WIDTH - 1, 0)], dimension_numbers=("NWC", "WIO", "NWC"),
        feature_group_count=ch)
    return out + b.astype(u.dtype)


def segsum(a):
    t = a.shape[-1]
    rep = jnp.broadcast_to(a[..., :, None], a.shape + (t,))
    strict = jnp.tril(jnp.ones((t, t), dtype=bool), -1)
    cs = jnp.cumsum(jnp.where(strict, rep, 0.0), axis=-2)
    return jnp.where(jnp.tril(jnp.ones((t, t), dtype=bool)), cs, -jnp.inf)


def ssd_mixer(z, xbc, dt_raw, conv_w, conv_b, dt_bias, a_log, d_skip, norm_g):
    out_dtype = z.dtype
    bsz, seq, _ = xbc.shape
    nc = seq // CHUNK
    xbc = jax.nn.silu(causal_depthwise_conv(xbc, conv_w, conv_b)).astype(jnp.float32)
    xs, bm, cm = jnp.split(xbc, [D_SSM, D_SSM + SSM_GROUPS * SSM_STATE], axis=-1)
    x = xs.reshape(bsz, nc, CHUNK, SSM_GROUPS, SSM_HPG, SSM_HEAD_DIM)
    bm = bm.reshape(bsz, nc, CHUNK, SSM_GROUPS, SSM_STATE)
    cm = cm.reshape(bsz, nc, CHUNK, SSM_GROUPS, SSM_STATE)
    dt = jax.nn.softplus(dt_raw.astype(jnp.float32) + dt_bias.astype(jnp.float32))
    dt = dt.reshape(bsz, nc, CHUNK, SSM_GROUPS, SSM_HPG)
    a = -jnp.exp(a_log.astype(jnp.float32)).reshape(SSM_GROUPS, SSM_HPG)
    da = jnp.transpose(dt * a, (0, 3, 4, 1, 2))
    xdt = x * dt[..., None]
    a_cs = jnp.cumsum(da, axis=-1)
    lmat = jnp.exp(segsum(da))
    cb = jnp.einsum("bclgn,bcsgn->bcgls", cm, bm)
    y_diag = jnp.einsum("bcgls,bgrcls,bcsgrp->bclgrp", cb, lmat, xdt)
    decay_states = jnp.exp(a_cs[..., -1:] - a_cs)
    states = jnp.einsum("bclgn,bgrcl,bclgrp->bcgrpn", bm, decay_states, xdt)
    chunk_decay = jnp.exp(a_cs[..., -1])

    def step(carry, inp):
        s_c, d_c = inp
        return carry * d_c[..., None, None] + s_c, carry

    init = jnp.zeros(states.shape[:1] + states.shape[2:], jnp.float32)
    _, prev = lax.scan(step, init, (jnp.moveaxis(states, 1, 0), jnp.moveaxis(chunk_decay, -1, 0)))
    prev = jnp.moveaxis(prev, 0, 1)
    y_off = jnp.einsum("bclgn,bcgrpn,bgrcl->bclgrp", cm, prev, jnp.exp(a_cs))
    y = y_diag + y_off + x * d_skip.astype(jnp.float32).reshape(SSM_GROUPS, SSM_HPG, 1)
    y = y.reshape(bsz, seq, D_SSM) * jax.nn.silu(z.astype(jnp.float32))
    y = rms_norm(y.reshape(bsz, seq, SSM_GROUPS, D_SSM // SSM_GROUPS),
                 norm_g.reshape(SSM_GROUPS, D_SSM // SSM_GROUPS))
    return y.reshape(bsz, seq, D_SSM).astype(out_dtype)


def rope_tables(positions):
    inv = ROPE_THETA ** (-jnp.arange(0, DIFF_HEAD_DIM, 2, dtype=jnp.float32) / DIFF_HEAD_DIM)
    ang = positions.astype(jnp.float32)[..., None] * inv
    return jnp.cos(ang), jnp.sin(ang)


def rope(t, cos, sin):
    half = t.shape[-1] // 2
    t1, t2 = t[..., :half], t[..., half:]
    return jnp.concatenate([t1 * cos - t2 * sin, t2 * cos + t1 * sin], axis=-1)


def diff_attention(q, k, v, cos, sin, q_norm_g, k_norm_g, lq1, lk1, lq2, lk2, subln_g, lam_init):
    dtype = q.dtype
    bsz, seq, _ = q.shape
    q = rms_norm(q.reshape(bsz, seq, DIFF_HEADS, 2, DIFF_HEAD_DIM), q_norm_g).astype(jnp.float32)
    k = rms_norm(k.reshape(bsz, seq, DIFF_HEADS, 2, DIFF_HEAD_DIM), k_norm_g).astype(jnp.float32)
    cs, sn = cos[:, :, None, None, :], sin[:, :, None, None, :]
    q = (rope(q, cs, sn) * DIFF_HEAD_DIM ** -0.5).astype(dtype)
    k = rope(k, cs, sn).astype(dtype)
    v = v.reshape(bsz, seq, DIFF_HEADS, DIFF_V_DIM)
    f32 = jnp.float32
    lam = (jnp.exp(jnp.sum(lq1.astype(f32) * lk1.astype(f32)))
           - jnp.exp(jnp.sum(lq2.astype(f32) * lk2.astype(f32))) + lam_init)
    nb = seq // Q_BLOCK
    q_blocks = jnp.moveaxis(q.reshape(bsz, nb, Q_BLOCK, DIFF_HEADS, 2, DIFF_HEAD_DIM), 1, 0)
    key_pos = jnp.arange(seq)

    def attend_block(args):
        qi, blk = args
        s = jnp.einsum("bqhmd,bkhmd->bhmqk", qi, k).astype(f32)
        q_pos = blk * Q_BLOCK + jnp.arange(Q_BLOCK)
        s = jnp.where(key_pos[None, :] <= q_pos[:, None], s, -jnp.inf)
        p = jax.nn.softmax(s, axis=-1)
        a = p[:, :, 0] - lam * p[:, :, 1]
        return jnp.einsum("bhqk,bkhd->bqhd", a.astype(dtype), v)

    o = lax.map(attend_block, (q_blocks, jnp.arange(nb)))
    o = jnp.moveaxis(o, 0, 1).reshape(bsz, seq, DIFF_HEADS, DIFF_V_DIM)
    o = rms_norm(o, subln_g).astype(f32) * (1.0 - lam_init)
    return o.reshape(bsz, seq, D_ATTN).astype(dtype)


def clamped_swiglu(u):
    glu = jnp.minimum(u[..., ::2], SWIGLU_LIMIT)
    lin = jnp.clip(u[..., 1::2], -SWIGLU_LIMIT, SWIGLU_LIMIT)
    return glu * jax.nn.sigmoid(SWIGLU_ALPHA * glu) * (lin + 1.0)


def moe_ffn(h, w_router, b_router, w1, b1, w2, b2):
    bsz, seq, d = h.shape
    t = h.reshape(-1, d)
    logits = (t @ w_router + b_router).astype(jnp.float32)
    top_vals, top_idx = lax.top_k(logits, TOP_K)
    gates = jax.nn.softmax(top_vals, axis=-1)
    dense_gate = jnp.sum(jax.nn.one_hot(top_idx, N_EXPERTS, dtype=jnp.float32) * gates[..., None], axis=1)
    out = jnp.zeros((t.shape[0], d), jnp.float32)
    for e in range(N_EXPERTS):
        act = clamped_swiglu(t @ w1[e] + b1[e])
        y = act @ w2[e] + b2[e]
        out = out + dense_gate[:, e:e + 1] * y.astype(jnp.float32)
    return out.reshape(bsz, seq, d).astype(h.dtype)


def setup_inputs(seed: int = 0) -> dict:
    key = jax.random.key(seed)
    ks = jax.random.split(key, 26)
    nrm = jax.random.normal
    f32 = jnp.float32
    x = nrm(ks[0], (BATCH, SEQ, D_MODEL), f32)
    positions = jnp.broadcast_to(jnp.arange(SEQ, dtype=jnp.int32), (BATCH, SEQ))
    norm1_g = 1.0 + 0.02 * nrm(ks[1], (DEPTH, D_MODEL), f32)
    w_in = nrm(ks[2], (DEPTH, D_MODEL, IN_COLS), f32) * D_MODEL ** -0.5
    conv_w = nrm(ks[3], (DEPTH, CONV_WIDTH, CONV_CH), f32) * CONV_WIDTH ** -0.5
    conv_b = 0.01 * nrm(ks[4], (DEPTH, CONV_CH), f32)
    u = jax.random.uniform(ks[5], (DEPTH, SSM_HEADS), f32)
    dt0 = jnp.maximum(jnp.exp(u * (math.log(DT_MAX) - math.log(DT_MIN)) + math.log(DT_MIN)), 1e-4)
    dt_bias = dt0 + jnp.log(-jnp.expm1(-dt0))
    a_log = jnp.log(jax.random.uniform(ks[6], (DEPTH, SSM_HEADS), f32, minval=1.0, maxval=16.0))
    d_skip = 1.0 + 0.1 * nrm(ks[7], (DEPTH, SSM_HEADS), f32)
    ssm_norm_g = 1.0 + 0.02 * nrm(ks[8], (DEPTH, D_SSM), f32)
    q_norm_g = 1.0 + 0.02 * nrm(ks[9], (DEPTH, DIFF_HEAD_DIM), f32)
    k_norm_g = 1.0 + 0.02 * nrm(ks[10], (DEPTH, DIFF_HEAD_DIM), f32)
    lambda_q1 = 0.1 * nrm(ks[11], (DEPTH, DIFF_HEAD_DIM), f32)
    lambda_k1 = 0.1 * nrm(ks[12], (DEPTH, DIFF_HEAD_DIM), f32)
    lambda_q2 = 0.1 * nrm(ks[13], (DEPTH, DIFF_HEAD_DIM), f32)
    lambda_k2 = 0.1 * nrm(ks[14], (DEPTH, DIFF_HEAD_DIM), f32)
    subln_g = 1.0 + 0.02 * nrm(ks[15], (DEPTH, DIFF_V_DIM), f32)
    w_out = nrm(ks[16], (DEPTH, D_MIX, D_MODEL), f32) * D_MIX ** -0.5
    norm2_g = 1.0 + 0.02 * nrm(ks[17], (DEPTH, D_MODEL), f32)
    w_router = nrm(ks[18], (DEPTH, D_MODEL, N_EXPERTS), f32) * D_MODEL ** -0.5
    b_router = 0.01 * nrm(ks[19], (DEPTH, N_EXPERTS), f32)
    w1 = nrm(ks[20], (DEPTH, N_EXPERTS, D_MODEL, 2 * D_EXPERT), f32) * D_MODEL ** -0.5
    b1 = 0.01 * nrm(ks[21], (DEPTH, N_EXPERTS, 2 * D_EXPERT), f32)
    w2 = nrm(ks[22], (DEPTH, N_EXPERTS, D_EXPERT, D_MODEL), f32) * D_EXPERT ** -0.5
    b2 = 0.01 * nrm(ks[23], (DEPTH, N_EXPERTS, D_MODEL), f32)
    return {"x": x, "positions": positions, "norm1_g": norm1_g, "w_in": w_in,
            "conv_w": conv_w, "conv_b": conv_b, "dt_bias": dt_bias, "a_log": a_log,
            "d_skip": d_skip, "ssm_norm_g": ssm_norm_g, "q_norm_g": q_norm_g,
            "k_norm_g": k_norm_g, "lambda_q1": lambda_q1, "lambda_k1": lambda_k1,
            "lambda_q2": lambda_q2, "lambda_k2": lambda_k2, "subln_g": subln_g,
            "w_out": w_out, "norm2_g": norm2_g, "w_router": w_router, "b_router": b_router,
            "w1": w1, "b1": b1, "w2": w2, "b2": b2}


def reference(x, positions, norm1_g, w_in, conv_w, conv_b, dt_bias, a_log, d_skip, ssm_norm_g,
              q_norm_g, k_norm_g, lambda_q1, lambda_k1, lambda_q2, lambda_k2, subln_g, w_out,
              norm2_g, w_router, b_router, w1, b1, w2, b2):
    cos, sin = rope_tables(positions)
    h = x
    for layer in range(DEPTH):
        lam_init = 0.8 - 0.6 * math.exp(-0.3 * layer)
        u = rms_norm(h, norm1_g[layer])
        proj = jnp.einsum("bld,dc->blc", u, w_in[layer])
        z, xbc, dt_raw, q, k, v = jnp.split(proj, IN_OFFSETS, axis=-1)
        y_ssm = ssd_mixer(z, xbc, dt_raw, conv_w[layer], conv_b[layer], dt_bias[layer],
                          a_log[layer], d_skip[layer], ssm_norm_g[layer])
        y_attn = diff_attention(q, k, v, cos, sin, q_norm_g[layer], k_norm_g[layer],
                                lambda_q1[layer], lambda_k1[layer], lambda_q2[layer],
                                lambda_k2[layer], subln_g[layer], lam_init)
        mix = jnp.concatenate([y_ssm, y_attn], axis=-1)
        h = h + jnp.einsum("blc,cd->bld", mix, w_out[layer])
        h = h + moe_ffn(rms_norm(h, norm2_g[layer]), w_router[layer], b_router[layer],
                        w1[layer], b1[layer], w2[layer], b2[layer])
    return h
```

```python
import functools
import math

import jax
import jax.numpy as jnp
from jax import lax
from jax.experimental import pallas as pl
from jax.experimental.pallas import tpu as pltpu

F32 = jnp.float32
BF16 = jnp.bfloat16
U32 = jnp.uint32
I32 = jnp.int32
HIGHEST = lax.Precision.HIGHEST

D_MODEL = 2048
D_SSM = 1024
D_ATTN = 1024
SSM_HEAD_DIM = 64
SSM_HEADS = 16
SSM_GROUPS = 2
SSM_STATE = 128
CONV_WIDTH = 4
CHUNK = 128
DIFF_HEAD_DIM = 64
DIFF_V_DIM = 128
DIFF_HEADS = 8
D_QK = 1024
D_V = 1024
ROPE_THETA = 10000.0
N_EXPERTS = 32
TOP_K = 4
D_EXPERT = 2048
SWIGLU_ALPHA = 1.702
SWIGLU_LIMIT = 7.0
NORM_EPS = 1e-5
LANES = 128
SUBLANES = 8
HALF = D_MODEL // 2

BC_W = SSM_GROUPS * SSM_STATE
PROJ_COLS = D_SSM + D_SSM + D_QK + D_QK + D_V + 2 * BC_W + LANES

MOE_TM = 256
VMEM_LIMIT = 56 * 1024 * 1024


def _cparams(n_axes, vmem=VMEM_LIMIT):
    return pltpu.CompilerParams(dimension_semantics=("arbitrary",) * n_axes, vmem_limit_bytes=vmem)


def _sigmoid(v):
    return 1.0 / (1.0 + jnp.exp(-v))


def _pack2(lo, hi):
    return pltpu.pack_elementwise([lo, hi], packed_dtype=BF16)


def _unpack2(w, index):
    return pltpu.unpack_elementwise(w, index=index, packed_dtype=BF16, unpacked_dtype=F32)


def _inproj_body(x_ref, g_ref, w_ref, o_ref, u_ref):
    @pl.when(pl.program_id(1) == 0)
    def _():
        x = x_ref[...]
        ms = jnp.mean(x * x, axis=-1, keepdims=True)
        u_ref[...] = (x * lax.rsqrt(ms + NORM_EPS) * g_ref[...]).astype(BF16)

    o_ref[...] = jnp.dot(u_ref[...], w_ref[...], preferred_element_type=F32)


def _in_proj(x2, g, w_p):
    t, d = x2.shape
    ncol = w_p.shape[1]
    tm, tn = 512, 640
    return pl.pallas_call(
        _inproj_body,
        grid=(t // tm, ncol // tn),
        in_specs=[pl.BlockSpec((tm, d), lambda i, j: (i, 0)),
                  pl.BlockSpec((1, d), lambda i, j: (0, 0)),
                  pl.BlockSpec((d, tn), lambda i, j: (0, j))],
        out_specs=pl.BlockSpec((tm, tn), lambda i, j: (i, j)),
        out_shape=jax.ShapeDtypeStruct((t, ncol), F32),
        scratch_shapes=[pltpu.VMEM((tm, d), BF16)],
        compiler_params=_cparams(2),
        name="in_proj",
    )(x2, g, w_p)


def _ssd_body(z_ref, x_ref, b_ref, c_ref, dt_ref, cwx_ref, cwb_ref, cwc_ref, cbx_ref, cbb_ref, cbc_ref,
              dtb_ref, alog_ref, dsk_ref, ng_ref, e_ref, y_ref, tx_ref, tb_ref, tc_ref, st_ref):
    @pl.when(pl.program_id(1) == 0)
    def _():
        tx_ref[...] = jnp.zeros_like(tx_ref)
        tb_ref[...] = jnp.zeros_like(tb_ref)
        tc_ref[...] = jnp.zeros_like(tc_ref)
        st_ref[...] = jnp.zeros_like(st_ref)

    row8 = lax.broadcasted_iota(I32, (SUBLANES, 1), 0)

    def conv_silu(u_ref, t_ref, w_ref, bias_ref):
        u = u_ref[...]
        tail = t_ref[...]
        w = w_ref[...]
        acc = u * w[CONV_WIDTH - 1:CONV_WIDTH, :] + bias_ref[...]
        for s in range(1, CONV_WIDTH):
            ru = pltpu.roll(u, s, 0)
            rt = pltpu.roll(tail, s, 0)
            head = jnp.where(row8 < s, rt, ru[0:SUBLANES, :])
            sh = jnp.concatenate([head, ru[SUBLANES:, :]], axis=0)
            acc = acc + sh * w[CONV_WIDTH - 1 - s:CONV_WIDTH - s, :]
        t_ref[...] = u[CHUNK - SUBLANES:CHUNK, :]
        return acc * _sigmoid(acc)

    xs = conv_silu(x_ref, tx_ref, cwx_ref, cbx_ref)
    bm = conv_silu(b_ref, tb_ref, cwb_ref, cbb_ref)
    cm = conv_silu(c_ref, tc_ref, cwc_ref, cbc_ref)

    dtr = dt_ref[...] + dtb_ref[...]
    dt = jnp.maximum(dtr, 0.0) + jnp.log1p(jnp.exp(-jnp.abs(dtr)))
    da = dt * (-jnp.exp(alog_ref[...]))

    rowi = lax.broadcasted_iota(I32, (CHUNK, 1), 0)
    acs = da
    k = 1
    while k < CHUNK:
        acs = acs + jnp.where(rowi >= k, pltpu.roll(acs, k, 0), 0.0)
        k *= 2
    acs_t = acs.T

    ex = jnp.dot(jnp.concatenate([dt, acs], axis=0), e_ref[...], precision=HIGHEST,
                 preferred_element_type=F32)
    dt_f = ex[:CHUNK]
    a_f = ex[CHUNK:]
    a_last = a_f[CHUNK - 1:CHUNK, :]
    expa = jnp.exp(a_f)
    dstate = jnp.exp(a_last - a_f)
    cdec = jnp.exp(a_last)

    xdt = xs * dt_f
    xdt_b = xdt.astype(BF16)
    xds_b = (xdt * dstate).astype(BF16)

    coli = lax.broadcasted_iota(I32, (1, CHUNK), 1)
    tril = rowi >= coli
    lane = lax.broadcasted_iota(I32, (1, LANES), 1)
    hpg = SSM_HEADS // SSM_GROUPS
    gw = hpg * SSM_HEAD_DIM
    y_groups = []
    for g in range(SSM_GROUPS):
        bm_g = bm[:, g * SSM_STATE:(g + 1) * SSM_STATE]
        cm_g = cm[:, g * SSM_STATE:(g + 1) * SSM_STATE]
        cm_b = cm_g.astype(BF16)
        cb = lax.dot_general(cm_b, bm_g.astype(BF16), (((1,), (1,)), ((), ())),
                             preferred_element_type=F32)
        parts = []
        for pp in range(hpg // 2):
            p = g * (hpg // 2) + pp
            xp = xdt_b[:, p * LANES:(p + 1) * LANES]
            ys = []
            for hh in range(2):
                h = 2 * p + hh
                seg = acs[:, h:h + 1] - acs_t[h:h + 1, :]
                lm = jnp.where(tril, jnp.exp(seg), 0.0)
                ys.append(jnp.dot((cb * lm).astype(BF16), xp, preferred_element_type=F32))
            parts.append(jnp.where(lane < SSM_HEAD_DIM, ys[0], ys[1]))
        y_diag = jnp.concatenate(parts, axis=1)
        st_g = st_ref[:, g * gw:(g + 1) * gw]
        y_off = jnp.dot(cm_b, st_g.astype(BF16), preferred_element_type=F32) * expa[:, g * gw:(g + 1) * gw]
        new = jnp.dot(bm_g.T.astype(BF16), xds_b[:, g * gw:(g + 1) * gw], preferred_element_type=F32)
        st_ref[:, g * gw:(g + 1) * gw] = st_g * cdec[:, g * gw:(g + 1) * gw] + new
        y_groups.append(y_diag + y_off)

    zz = z_ref[...]
    gate = zz * _sigmoid(zz)
    dsk = dsk_ref[...]
    ng = ng_ref[...]
    outs = []
    for g in range(SSM_GROUPS):
        sl = slice(g * gw, (g + 1) * gw)
        yg = (y_groups[g] + xs[:, sl] * dsk[:, sl]) * gate[:, sl]
        ms = jnp.mean(yg * yg, axis=-1, keepdims=True)
        outs.append(yg * lax.rsqrt(ms + NORM_EPS) * ng[:, sl])
    y_ref[...] = jnp.concatenate(outs, axis=1).astype(BF16)


def _ssd(proj, bsz, seq, cw, cb, dt_bias, a_log, d_skip, norm_g):
    t = bsz * seq
    nc = seq // CHUNK
    cwx, cwb, cwc = cw[:, :D_SSM], cw[:, D_SSM:D_SSM + BC_W], cw[:, D_SSM + BC_W:]
    cb = cb[None, :]
    cbx, cbb, cbc = cb[:, :D_SSM], cb[:, D_SSM:D_SSM + BC_W], cb[:, D_SSM + BC_W:]
    pad = LANES - SSM_HEADS
    dtb = jnp.pad(dt_bias, (0, pad))[None, :]
    alog = jnp.pad(a_log, (0, pad))[None, :]
    dsk = jnp.repeat(d_skip, SSM_HEAD_DIM)[None, :]
    expand = (jnp.arange(D_SSM)[None, :] // SSM_HEAD_DIM == jnp.arange(LANES)[:, None]).astype(F32)

    def row(b, c):
        return b * nc + c

    def full(shape):
        return pl.BlockSpec(shape, lambda b, c: (0, 0))

    bc0 = (2 * D_SSM + 2 * D_QK + D_V) // BC_W
    dt0 = (PROJ_COLS - LANES) // LANES
    return pl.pallas_call(
        _ssd_body,
        grid=(bsz, nc),
        in_specs=[pl.BlockSpec((CHUNK, D_SSM), lambda b, c: (row(b, c), 0)),
                  pl.BlockSpec((CHUNK, D_SSM), lambda b, c: (row(b, c), 1)),
                  pl.BlockSpec((CHUNK, BC_W), lambda b, c: (row(b, c), bc0)),
                  pl.BlockSpec((CHUNK, BC_W), lambda b, c: (row(b, c), bc0 + 1)),
                  pl.BlockSpec((CHUNK, LANES), lambda b, c: (row(b, c), dt0)),
                  full((CONV_WIDTH, D_SSM)), full((CONV_WIDTH, BC_W)), full((CONV_WIDTH, BC_W)),
                  full((1, D_SSM)), full((1, BC_W)), full((1, BC_W)),
                  full((1, LANES)), full((1, LANES)), full((1, D_SSM)), full((1, D_SSM)),
                  full((LANES, D_SSM))],
        out_specs=pl.BlockSpec((CHUNK, D_SSM), lambda b, c: (row(b, c), 0)),
        out_shape=jax.ShapeDtypeStruct((t, D_SSM), BF16),
        scratch_shapes=[pltpu.VMEM((SUBLANES, D_SSM), F32), pltpu.VMEM((SUBLANES, BC_W), F32),
                        pltpu.VMEM((SUBLANES, BC_W), F32), pltpu.VMEM((SSM_STATE, D_SSM), F32)],
        compiler_params=_cparams(2),
        name="ssd",
    )(proj, proj, proj, proj, proj, cwx, cwb, cwc, cbx, cbb, cbc, dtb, alog, dsk, norm_g[None, :], expand)


def _qkprep_body(q_ref, k_ref, v_ref, cos_ref, sin_ref, gq_ref, gk_ref, gm_ref, qo_ref, ko_ref, vo_ref):
    reps = D_QK // LANES
    cos = jnp.tile(cos_ref[...], (1, reps))
    sin = jnp.tile(sin_ref[...], (1, reps))
    lane = lax.broadcasted_iota(I32, (1, D_QK), 1)
    first = (lane & (DIFF_HEAD_DIM - 1)) < DIFF_HEAD_DIM // 2
    gm = gm_ref[...]

    def prep(t, g, scale):
        sq = t * t
        hi = sq.astype(BF16)
        lo = (sq - hi.astype(F32)).astype(BF16)
        ss = jnp.dot(hi, gm, preferred_element_type=F32) + jnp.dot(lo, gm, preferred_element_type=F32)
        tn = t * lax.rsqrt(ss * (1.0 / DIFF_HEAD_DIM) + NORM_EPS) * g
        half = DIFF_HEAD_DIM // 2
        rot = jnp.where(first, pltpu.roll(tn, D_QK - half, 1), pltpu.roll(tn, half, 1))
        return (tn * cos + rot * sin) * scale

    qo_ref[...] = prep(q_ref[...], gq_ref[...], DIFF_HEAD_DIM ** -0.5).astype(BF16)
    ko_ref[...] = prep(k_ref[...], gk_ref[...], 1.0).astype(BF16)
    vo_ref[...] = v_ref[...].astype(BF16)


def _qk_prep(proj, cos_t, sin_t, gq, gk):
    t = proj.shape[0]
    tm = 256
    gsum = (jnp.arange(D_QK)[:, None] // DIFF_HEAD_DIM == jnp.arange(D_QK)[None, :] // DIFF_HEAD_DIM).astype(BF16)
    out = jax.ShapeDtypeStruct((t, D_QK), BF16)
    return pl.pallas_call(
        _qkprep_body,
        grid=(t // tm,),
        in_specs=[pl.BlockSpec((tm, D_QK), lambda i: (i, 2)),
                  pl.BlockSpec((tm, D_QK), lambda i: (i, 3)),
                  pl.BlockSpec((tm, D_V), lambda i: (i, 4)),
                  pl.BlockSpec((tm, LANES), lambda i: (i, 0)),
                  pl.BlockSpec((tm, LANES), lambda i: (i, 0)),
                  pl.BlockSpec((1, D_QK), lambda i: (0, 0)),
                  pl.BlockSpec((1, D_QK), lambda i: (0, 0)),
                  pl.BlockSpec((D_QK, D_QK), lambda i: (0, 0))],
        out_specs=[pl.BlockSpec((tm, D_QK), lambda i: (i, 0))] * 3,
        out_shape=[out, out, out],
        compiler_params=_cparams(1),
        name="qk_prep",
    )(proj, proj, proj, cos_t, sin_t, gq, gk, gsum)


def _attn_body(lam_ref, q_ref, k_ref, v_ref, sg_ref, o_ref, qs_ref, m_ref, l_ref, acc_ref, *, tq, tk, out_scale):
    i = pl.program_id(2)
    j = pl.program_id(3)

    @pl.when(j == 0)
    def _():
        q = q_ref[...]
        lane = lax.broadcasted_iota(I32, (1, LANES), 1)
        zero = jnp.zeros_like(q)
        qs_ref[0:tq, :] = jnp.where(lane < DIFF_HEAD_DIM, q, zero)
        qs_ref[tq:2 * tq, :] = jnp.where(lane >= DIFF_HEAD_DIM, q, zero)
        m_ref[...] = jnp.full_like(m_ref, -jnp.inf)
        l_ref[...] = jnp.zeros_like(l_ref)
        acc_ref[...] = jnp.zeros_like(acc_ref)

    def step(masked):
        s = lax.dot_general(qs_ref[...], k_ref[...], (((1,), (1,)), ((), ())), preferred_element_type=F32)
        if masked:
            r = lax.broadcasted_iota(I32, (2 * tq, tk), 0)
            c = lax.broadcasted_iota(I32, (2 * tq, tk), 1)
            s = jnp.where(c <= jnp.where(r >= tq, r - tq, r), s, -jnp.inf)
        m_prev = m_ref[...]
        m_new = jnp.maximum(m_prev, jnp.max(s, axis=1, keepdims=True))
        alpha = jnp.exp(m_prev - m_new)
        p = jnp.exp(s - jnp.tile(m_new, (1, tk // LANES)))
        l_ref[...] = alpha * l_ref[...] + jnp.sum(p, axis=1, keepdims=True)
        acc_ref[...] = alpha * acc_ref[...] + jnp.dot(p.astype(BF16), v_ref[...], preferred_element_type=F32)
        m_ref[...] = m_new

    @pl.when(j < i)
    def _():
        step(False)

    @pl.when(j == i)
    def _():
        step(True)
        o = acc_ref[...] / l_ref[...]
        d = o[0:tq, :] - lam_ref[0] * o[tq:2 * tq, :]
        ms = jnp.mean(d * d, axis=-1, keepdims=True)
        o_ref[...] = (d * lax.rsqrt(ms + NORM_EPS) * sg_ref[...] * out_scale).astype(BF16)


def _attention(qr, kr, vb, lam, subln_g, bsz, seq, out_scale):
    t = bsz * seq
    tq = tk = 512
    nq = seq // tq
    body = functools.partial(_attn_body, tq=tq, tk=tk, out_scale=out_scale)
    grid_spec = pltpu.PrefetchScalarGridSpec(
        num_scalar_prefetch=1,
        grid=(bsz, DIFF_HEADS, nq, nq),
        in_specs=[pl.BlockSpec((tq, LANES), lambda b, h, i, j, lam: (b * nq + i, h)),
                  pl.BlockSpec((tk, LANES), lambda b, h, i, j, lam: (b * nq + jnp.minimum(j, i), h)),
                  pl.BlockSpec((tk, LANES), lambda b, h, i, j, lam: (b * nq + jnp.minimum(j, i), h)),
                  pl.BlockSpec((1, LANES), lambda b, h, i, j, lam: (0, 0))],
        out_specs=pl.BlockSpec((tq, LANES), lambda b, h, i, j, lam: (b * nq + i, h)),
        scratch_shapes=[pltpu.VMEM((2 * tq, LANES), BF16), pltpu.VMEM((2 * tq, LANES), F32),
                        pltpu.VMEM((2 * tq, LANES), F32), pltpu.VMEM((2 * tq, LANES), F32)],
    )
    return pl.pallas_call(
        body, grid_spec=grid_spec,
        out_shape=jax.ShapeDtypeStruct((t, D_ATTN), BF16),
        compiler_params=_cparams(4),
        name="diff_attention",
    )(lam, qr, kr, vb, subln_g[None, :])


def _outproj_body(x_ref, ys_ref, ya_ref, wt_ref, wb_ref, g2_ref, wr_ref, br_ref, ltri_ref,
                  h1_ref, hp_ref, ti_ref, tg_ref, cnt_out_ref, cnt_ref):
    h1 = (x_ref[...] + jnp.dot(ys_ref[...], wt_ref[...], preferred_element_type=F32)
          + jnp.dot(ya_ref[...], wb_ref[...], preferred_element_type=F32))
    h1_ref[...] = h1
    ms = jnp.mean(h1 * h1, axis=-1, keepdims=True)
    h2 = h1 * lax.rsqrt(ms + NORM_EPS) * g2_ref[...]
    hp_ref[...] = _pack2(h2[:, :HALF], h2[:, HALF:])

    logits = jnp.dot(h2, wr_ref[...], precision=HIGHEST, preferred_element_type=F32) + br_ref[...]
    lane = lax.broadcasted_iota(I32, (1, LANES), 1)
    lanef = lane.astype(F32)
    cur = jnp.where(lane < N_EXPERTS, logits, -jnp.inf)
    vals, idxs = [], []
    for _ in range(TOP_K):
        m = jnp.max(cur, axis=1, keepdims=True)
        idx = jnp.min(jnp.where(cur == m, lanef, float(LANES)), axis=1, keepdims=True)
        vals.append(m)
        idxs.append(idx)
        cur = jnp.where(lanef == idx, -jnp.inf, cur)
    es = [jnp.exp(v - vals[0]) for v in vals]
    den = es[0] + es[1] + es[2] + es[3]

    @pl.when(pl.program_id(0) == 0)
    def _():
        cnt_ref[...] = jnp.zeros_like(cnt_ref)

    base = cnt_ref[...]
    ltri = ltri_ref[...]
    ti = jnp.zeros(logits.shape, F32)
    tg = jnp.zeros(logits.shape, F32)
    for kk in range(TOP_K):
        oh = lanef == idxs[kk]
        ohf = jnp.where(oh, 1.0, 0.0)
        pre = jnp.dot(ltri, ohf.astype(BF16), preferred_element_type=F32)
        rank = jnp.sum(jnp.where(oh, pre + base, 0.0), axis=1, keepdims=True)
        base = base + jnp.sum(ohf, axis=0, keepdims=True)
        ti = jnp.where(lane == kk, idxs[kk], ti)
        ti = jnp.where(lane == TOP_K + kk, rank, ti)
        tg = jnp.where(lane == kk, es[kk] / den, tg)
    cnt_ref[...] = base
    ti_ref[...] = ti.astype(I32)
    tg_ref[...] = tg
    cnt_out_ref[...] = jnp.broadcast_to(base, cnt_out_ref.shape).astype(I32)


def _out_proj_router(x2, y_ssm, y_attn, w_top, w_bot, g2, w_r, b_r):
    t, d = x2.shape
    tm = 256
    ltri = (jnp.arange(tm)[:, None] > jnp.arange(tm)[None, :]).astype(BF16)

    def full(shape):
        return pl.BlockSpec(shape, lambda i: (0, 0))

    return pl.pallas_call(
        _outproj_body,
        grid=(t // tm,),
        in_specs=[pl.BlockSpec((tm, d), lambda i: (i, 0)),
                  pl.BlockSpec((tm, D_SSM), lambda i: (i, 0)),
                  pl.BlockSpec((tm, D_ATTN), lambda i: (i, 0)),
                  full((D_SSM, d)), full((D_ATTN, d)), full((1, d)), full((d, LANES)), full((1, LANES)),
                  full((tm, tm))],
        out_specs=[pl.BlockSpec((tm, d), lambda i: (i, 0)),
                   pl.BlockSpec((tm, HALF), lambda i: (i, 0)),
                   pl.BlockSpec((tm, LANES), lambda i: (i, 0)),
                   pl.BlockSpec((tm, LANES), lambda i: (i, 0)),
                   full((SUBLANES, LANES))],
        out_shape=[jax.ShapeDtypeStruct((t, d), F32), jax.ShapeDtypeStruct((t, HALF), U32),
                   jax.ShapeDtypeStruct((t, LANES), I32), jax.ShapeDtypeStruct((t, LANES), F32),
                   jax.ShapeDtypeStruct((SUBLANES, LANES), I32)],
        scratch_shapes=[pltpu.VMEM((1, LANES), F32)],
        compiler_params=_cparams(1),
        name="out_proj_router",
    )(x2, y_ssm, y_attn, w_top, w_bot, g2, w_r, b_r, ltri)


def _row_copy(src_hbm, src_row, dst_ref, dst_row, sem):
    return pltpu.make_async_copy(src_hbm.at[pl.ds(src_row, 1), :], dst_ref.at[pl.ds(dst_row, 1), :], sem)


def _gather_body(idx_ref, src_hbm, o_ref, sem, *, rows):
    def start(r, carry):
        _row_copy(src_hbm, idx_ref[0, 0, r], o_ref, r, sem).start()
        return carry

    def wait(r, carry):
        _row_copy(src_hbm, 0, o_ref, r, sem).wait()
        return carry

    lax.fori_loop(0, rows, start, 0)
    lax.fori_loop(0, rows, wait, 0)


def _gather_rows(src, row_ids, rows_per_step):
    r = row_ids.shape[0]
    w = src.shape[1]
    nsteps = r // rows_per_step
    return pl.pallas_call(
        functools.partial(_gather_body, rows=rows_per_step),
        grid=(nsteps,),
        in_specs=[pl.BlockSpec((1, 1, rows_per_step), lambda i: (i, 0, 0), memory_space=pltpu.SMEM),
                  pl.BlockSpec(memory_space=pl.ANY)],
        out_specs=pl.BlockSpec((rows_per_step, w), lambda i: (i, 0)),
        out_shape=jax.ShapeDtypeStruct((r, w), src.dtype),
        scratch_shapes=[pltpu.SemaphoreType.DMA(())],
        compiler_params=_cparams(1),
        name="gather_rows",
    )(row_ids.reshape(nsteps, 1, rows_per_step), src)


def _moe_up_body(te_ref, nv_ref, x_ref, wg_ref, wl_ref, bg_ref, bl_ref, o_ref):
    i = pl.program_id(1)

    @pl.when(i < nv_ref[0])
    def _():
        w = x_ref[...]
        lo = _unpack2(w, 0).astype(BF16)
        hi = _unpack2(w, 1).astype(BF16)

        def up(w_ref, b_ref):
            return (jnp.dot(lo, w_ref[0:HALF, :], preferred_element_type=F32)
                    + jnp.dot(hi, w_ref[HALF:, :], preferred_element_type=F32) + b_ref[...])

        glu = jnp.minimum(up(wg_ref, bg_ref), SWIGLU_LIMIT)
        lin = jnp.clip(up(wl_ref, bl_ref), -SWIGLU_LIMIT, SWIGLU_LIMIT)
        o_ref[...] = (glu * _sigmoid(SWIGLU_ALPHA * glu) * (lin + 1.0)).astype(BF16)

    @pl.when(i >= nv_ref[0])
    def _():
        o_ref[...] = jnp.zeros_like(o_ref)


def _moe_up(tile_expert, n_valid, xs, w1g, w1l, b1g, b1l):
    r = xs.shape[0]
    tm, tn = MOE_TM, 1024
    nt = r // tm
    grid_spec = pltpu.PrefetchScalarGridSpec(
        num_scalar_prefetch=2,
        grid=(D_EXPERT // tn, nt),
        in_specs=[pl.BlockSpec((tm, HALF), lambda n, i, te, nv: (i, 0)),
                  pl.BlockSpec((None, D_MODEL, tn), lambda n, i, te, nv: (te[i], 0, n)),
                  pl.BlockSpec((None, D_MODEL, tn), lambda n, i, te, nv: (te[i], 0, n)),
                  pl.BlockSpec((None, 1, tn), lambda n, i, te, nv: (te[i], 0, n)),
                  pl.BlockSpec((None, 1, tn), lambda n, i, te, nv: (te[i], 0, n))],
        out_specs=pl.BlockSpec((tm, tn), lambda n, i, te, nv: (i, n)),
    )
    return pl.pallas_call(
        _moe_up_body, grid_spec=grid_spec,
        out_shape=jax.ShapeDtypeStruct((r, D_EXPERT), BF16),
        compiler_params=_cparams(2),
        name="moe_up",
    )(tile_expert, n_valid, xs, w1g, w1l, b1g, b1l)


def _moe_down_body(te_ref, nv_ref, a_ref, w_ref, b_ref, o_ref):
    i = pl.program_id(0)

    @pl.when(i < nv_ref[0])
    def _():
        y = jnp.dot(a_ref[...], w_ref[...], preferred_element_type=F32) + b_ref[...]
        o_ref[...] = _pack2(y[:, :HALF], y[:, HALF:])

    @pl.when(i >= nv_ref[0])
    def _():
        zero = jnp.zeros(o_ref.shape, F32)
        o_ref[...] = _pack2(zero, zero)


def _moe_down(tile_expert, n_valid, act, w2b, b2):
    r = act.shape[0]
    tm = MOE_TM
    nt = r // tm
    grid_spec = pltpu.PrefetchScalarGridSpec(
        num_scalar_prefetch=2,
        grid=(nt,),
        in_specs=[pl.BlockSpec((tm, D_EXPERT), lambda i, te, nv: (i, 0)),
                  pl.BlockSpec((None, D_EXPERT, D_MODEL), lambda i, te, nv: (te[i], 0, 0)),
                  pl.BlockSpec((None, 1, D_MODEL), lambda i, te, nv: (te[i], 0, 0))],
        out_specs=pl.BlockSpec((tm, HALF), lambda i, te, nv: (i, 0)),
    )
    return pl.pallas_call(
        _moe_down_body, grid_spec=grid_spec,
        out_shape=jax.ShapeDtypeStruct((r, HALF), U32),
        compiler_params=_cparams(1),
        name="moe_down",
    )(tile_expert, n_valid, act, w2b, b2)


def _combine_body(idx_ref, y_hbm, h1_ref, g_ref, o_ref, buf_ref, sem, *, tc):
    n = TOP_K * tc

    def start(a, carry):
        _row_copy(y_hbm, idx_ref[0, 0, a], buf_ref, a, sem).start()
        return carry

    def wait(a, carry):
        _row_copy(y_hbm, 0, buf_ref, a, sem).wait()
        return carry

    lax.fori_loop(0, n, start, 0)
    lax.fori_loop(0, n, wait, 0)

    g = g_ref[...]
    lo = h1_ref[:, :HALF]
    hi = h1_ref[:, HALF:]
    for kk in range(TOP_K):
        w = buf_ref[kk * tc:(kk + 1) * tc, :]
        gk = g[:, kk:kk + 1]
        lo = lo + gk * _unpack2(w, 0)
        hi = hi + gk * _unpack2(w, 1)
    o_ref[:, :HALF] = lo
    o_ref[:, HALF:] = hi


def _combine(pos_km, y_rows, h1, gates):
    t, d = h1.shape
    tc = 128
    nsteps = t // tc
    return pl.pallas_call(
        functools.partial(_combine_body, tc=tc),
        grid=(nsteps,),
        in_specs=[pl.BlockSpec((1, 1, TOP_K * tc), lambda i: (i, 0, 0), memory_space=pltpu.SMEM),
                  pl.BlockSpec(memory_space=pl.ANY),
                  pl.BlockSpec((tc, d), lambda i: (i, 0)),
                  pl.BlockSpec((tc, LANES), lambda i: (i, 0))],
        out_specs=pl.BlockSpec((tc, d), lambda i: (i, 0)),
        out_shape=jax.ShapeDtypeStruct((t, d), F32),
        scratch_shapes=[pltpu.VMEM((TOP_K * tc, HALF), U32), pltpu.SemaphoreType.DMA(())],
        compiler_params=_cparams(1),
        name="moe_combine",
    )(pos_km, y_rows, h1, gates)


def _route(ti, counts, tm):
    t = ti.shape[0]
    n_assign = t * TOP_K
    n_rows = n_assign + N_EXPERTS * tm
    nt = n_rows // tm
    top_e = ti[:, :TOP_K]
    rank = ti[:, TOP_K:2 * TOP_K]
    counts = counts[0, :N_EXPERTS]
    tiles_per = (counts + (tm - 1)) // tm
    tile_end = jnp.cumsum(tiles_per)
    pstart = (tile_end - tiles_per) * tm
    experts = jnp.arange(N_EXPERTS, dtype=I32)
    onehot = top_e[:, :, None] == experts[None, None, :]
    pos = jnp.sum(jnp.where(onehot, pstart[None, None, :], 0), axis=-1) + rank
    token_of = jnp.arange(n_assign, dtype=I32) // TOP_K
    row_token = jnp.zeros((n_rows,), I32).at[pos.reshape(-1)].set(token_of, mode="drop", unique_indices=True)
    n_valid = tile_end[N_EXPERTS - 1]
    tile_ids = jnp.minimum(jnp.arange(nt, dtype=I32), n_valid - 1)
    tile_expert = jnp.sum((tile_end[None, :] <= tile_ids[:, None]).astype(I32), axis=1)
    tile_expert = jnp.minimum(tile_expert, N_EXPERTS - 1)
    return row_token, pos, tile_expert, n_valid.reshape(1)


def _layer(h, cos_t, sin_t, lam_init, norm1_g, w_in, conv_w, conv_b, dt_bias, a_log, d_skip, ssm_norm_g,
           q_norm_g, k_norm_g, lq1, lk1, lq2, lk2, subln_g, w_out, norm2_g, w_router, b_router, w1, b1, w2, b2):
    bsz, seq, d = h.shape
    t = bsz * seq
    x2 = h.reshape(t, d)

    o_z = 0
    o_xbc = D_SSM
    o_dt = o_xbc + D_SSM + 2 * BC_W
    o_q = o_dt + SSM_HEADS
    o_k = o_q + D_QK
    o_v = o_k + D_QK
    w_p = jnp.concatenate([
        w_in[:, o_z:o_z + D_SSM], w_in[:, o_xbc:o_xbc + D_SSM],
        w_in[:, o_q:o_q + D_QK], w_in[:, o_k:o_k + D_QK], w_in[:, o_v:o_v + D_V],
        w_in[:, o_xbc + D_SSM:o_xbc + D_SSM + 2 * BC_W],
        jnp.pad(w_in[:, o_dt:o_dt + SSM_HEADS], ((0, 0), (0, LANES - SSM_HEADS)))], axis=1).astype(BF16)

    proj = _in_proj(x2, norm1_g[None, :], w_p)
    y_ssm = _ssd(proj, bsz, seq, conv_w, conv_b, dt_bias, a_log, d_skip, ssm_norm_g)

    reps = D_QK // DIFF_HEAD_DIM
    qr, kr, vb = _qk_prep(proj, cos_t, sin_t, jnp.tile(q_norm_g, reps)[None, :], jnp.tile(k_norm_g, reps)[None, :])
    lam = (jnp.exp(jnp.sum(lq1 * lk1)) - jnp.exp(jnp.sum(lq2 * lk2)) + lam_init).reshape(1).astype(F32)
    y_attn = _attention(qr, kr, vb, lam, subln_g, bsz, seq, 1.0 - lam_init)

    w_ob = w_out.astype(BF16)
    w_r = jnp.pad(w_router, ((0, 0), (0, LANES - N_EXPERTS)))
    b_r = jnp.pad(b_router, (0, LANES - N_EXPERTS))[None, :]
    h1, h2p, ti, tg, counts = _out_proj_router(x2, y_ssm, y_attn, w_ob[:D_SSM], w_ob[D_SSM:], norm2_g[None, :],
                                               w_r, b_r)

    row_token, pos, tile_expert, n_valid = _route(ti, counts, MOE_TM)
    xs = _gather_rows(h2p, row_token, MOE_TM)
    w1g = w1[:, :, 0::2].astype(BF16)
    w1l = w1[:, :, 1::2].astype(BF16)
    act = _moe_up(tile_expert, n_valid, xs, w1g, w1l, b1[:, None, 0::2], b1[:, None, 1::2])
    y_rows = _moe_down(tile_expert, n_valid, act, w2.astype(BF16), b2[:, None, :])

    tc = 128
    pos_km = pos.reshape(t // tc, tc, TOP_K).transpose(0, 2, 1).reshape(t // tc, 1, TOP_K * tc)
    out = _combine(pos_km, y_rows, h1, tg)
    return out.reshape(bsz, seq, d)


def kernel(x, positions, norm1_g, w_in, conv_w, conv_b, dt_bias, a_log, d_skip, ssm_norm_g, q_norm_g, k_norm_g,
           lambda_q1, lambda_k1, lambda_q2, lambda_k2, subln_g, w_out, norm2_g, w_router, b_router, w1, b1, w2, b2):
    bsz, seq, _ = x.shape
    inv = ROPE_THETA ** (-jnp.arange(0, DIFF_HEAD_DIM, 2, dtype=F32) / DIFF_HEAD_DIM)
    ang = positions.astype(F32).reshape(bsz * seq, 1) * inv[None, :]
    cos, sin = jnp.cos(ang), jnp.sin(ang)
    cos_t = jnp.tile(cos, (1, 2 * LANES // DIFF_HEAD_DIM))
    sin_t = jnp.tile(jnp.concatenate([-sin, sin], axis=1), (1, LANES // DIFF_HEAD_DIM))

    h = x
    for layer in range(norm1_g.shape[0]):
        lam_init = 0.8 - 0.6 * math.exp(-0.3 * layer)
        h = _layer(h, cos_t, sin_t, lam_init, norm1_g[layer], w_in[layer], conv_w[layer], conv_b[layer],
                   dt_bias[layer], a_log[layer], d_skip[layer], ssm_norm_g[layer], q_norm_g[layer],
                   k_norm_g[layer], lambda_q1[layer], lambda_k1[layer], lambda_q2[layer], lambda_k2[layer],
                   subln_g[layer], w_out[layer], norm2_g[layer], w_router[layer], b_router[layer],
                   w1[layer], b1[layer], w2[layer], b2[layer])
    return h
```

```python
import functools
import math

import jax
import jax.numpy as jnp
from jax import lax
from jax.experimental import pallas as pl
from jax.experimental.pallas import tpu as pltpu

F32 = jnp.float32
BF16 = jnp.bfloat16
U32 = jnp.uint32
I32 = jnp.int32
HIGHEST = lax.Precision.HIGHEST

D_MODEL = 2048
D_SSM = 1024
D_ATTN = 1024
SSM_HEAD_DIM = 64
SSM_HEADS = 16
SSM_GROUPS = 2
SSM_STATE = 128
CONV_WIDTH = 4
CHUNK = 128
DIFF_HEAD_DIM = 64
DIFF_V_DIM = 128
DIFF_HEADS = 8
D_QK = 1024
D_V = 1024
ROPE_THETA = 10000.0
N_EXPERTS = 32
TOP_K = 4
D_EXPERT = 2048
SWIGLU_ALPHA = 1.702
SWIGLU_LIMIT = 7.0
NORM_EPS = 1e-5
LANES = 128
SUBLANES = 8
HALF = D_MODEL // 2

BC_W = SSM_GROUPS * SSM_STATE
PROJ_COLS = D_SSM + D_SSM + D_QK + D_QK + D_V + 2 * BC_W + LANES

MOE_TM = 256
VMEM_LIMIT = 56 * 1024 * 1024


def _cparams(n_axes, vmem=VMEM_LIMIT):
    return pltpu.CompilerParams(dimension_semantics=("arbitrary",) * n_axes, vmem_limit_bytes=vmem)


def _sigmoid(v):
    return 1.0 / (1.0 + jnp.exp(-v))


def _pack2(lo, hi):
    return pltpu.pack_elementwise([lo, hi], packed_dtype=BF16)


def _unpack2(w, index):
    return pltpu.unpack_elementwise(w, index=index, packed_dtype=BF16, unpacked_dtype=F32)


def _inproj_body(x_ref, g_ref, w_ref, o_ref, u_ref):
    @pl.when(pl.program_id(1) == 0)
    def _():
        x = x_ref[...]
        ms = jnp.mean(x * x, axis=-1, keepdims=True)
        u_ref[...] = (x * lax.rsqrt(ms + NORM_EPS) * g_ref[...]).astype(BF16)

    o_ref[...] = jnp.dot(u_ref[...], w_ref[...], preferred_element_type=F32)


def _in_proj(x2, g, w_p):
    t, d = x2.shape
    ncol = w_p.shape[1]
    tm, tn = 512, 640
    return pl.pallas_call(
        _inproj_body,
        grid=(t // tm, ncol // tn),
        in_specs=[pl.BlockSpec((tm, d), lambda i, j: (i, 0)),
                  pl.BlockSpec((1, d), lambda i, j: (0, 0)),
                  pl.BlockSpec((d, tn), lambda i, j: (0, j))],
        out_specs=pl.BlockSpec((tm, tn), lambda i, j: (i, j)),
        out_shape=jax.ShapeDtypeStruct((t, ncol), F32),
        scratch_shapes=[pltpu.VMEM((tm, d), BF16)],
        compiler_params=_cparams(2),
        name="in_proj",
    )(x2, g, w_p)


def _ssd_body(z_ref, x_ref, b_ref, c_ref, dt_ref, cwx_ref, cwb_ref, cwc_ref, cbx_ref, cbb_ref, cbc_ref,
              dtb_ref, alog_ref, dsk_ref, ng_ref, e_ref, y_ref, tx_ref, tb_ref, tc_ref, st_ref):
    @pl.when(pl.program_id(1) == 0)
    def _():
        tx_ref[...] = jnp.zeros_like(tx_ref)
        tb_ref[...] = jnp.zeros_like(tb_ref)
        tc_ref[...] = jnp.zeros_like(tc_ref)
        st_ref[...] = jnp.zeros_like(st_ref)

    row8 = lax.broadcasted_iota(I32, (SUBLANES, 1), 0)

    def conv_silu(u_ref, t_ref, w_ref, bias_ref):
        u = u_ref[...]
        tail = t_ref[...]
        w = w_ref[...]
        acc = u * w[CONV_WIDTH - 1:CONV_WIDTH, :] + bias_ref[...]
        for s in range(1, CONV_WIDTH):
            ru = pltpu.roll(u, s, 0)
            rt = pltpu.roll(tail, s, 0)
            head = jnp.where(row8 < s, rt, ru[0:SUBLANES, :])
            sh = jnp.concatenate([head, ru[SUBLANES:, :]], axis=0)
            acc = acc + sh * w[CONV_WIDTH - 1 - s:CONV_WIDTH - s, :]
        t_ref[...] = u[CHUNK - SUBLANES:CHUNK, :]
        return acc * _sigmoid(acc)

    xs = conv_silu(x_ref, tx_ref, cwx_ref, cbx_ref)
    bm = conv_silu(b_ref, tb_ref, cwb_ref, cbb_ref)
    cm = conv_silu(c_ref, tc_ref, cwc_ref, cbc_ref)

    dtr = dt_ref[...] + dtb_ref[...]
    dt = jnp.maximum(dtr, 0.0) + jnp.log1p(jnp.exp(-jnp.abs(dtr)))
    da = dt * (-jnp.exp(alog_ref[...]))

    rowi = lax.broadcasted_iota(I32, (CHUNK, 1), 0)
    acs = da
    k = 1
    while k < CHUNK:
        acs = acs + jnp.where(rowi >= k, pltpu.roll(acs, k, 0), 0.0)
        k *= 2
    acs_t = acs.T

    ex = jnp.dot(jnp.concatenate([dt, acs], axis=0), e_ref[...], precision=HIGHEST,
                 preferred_element_type=F32)
    dt_f = ex[:CHUNK]
    a_f = ex[CHUNK:]
    a_last = a_f[CHUNK - 1:CHUNK, :]
    expa = jnp.exp(a_f)
    dstate = jnp.exp(a_last - a_f)
    cdec = jnp.exp(a_last)

    xdt = xs * dt_f
    xdt_b = xdt.astype(BF16)
    xds_b = (xdt * dstate).astype(BF16)

    coli = lax.broadcasted_iota(I32, (1, CHUNK), 1)
    tril = rowi >= coli
    lane = lax.broadcasted_iota(I32, (1, LANES), 1)
    hpg = SSM_HEADS // SSM_GROUPS
    gw = hpg * SSM_HEAD_DIM
    y_groups = []
    for g in range(SSM_GROUPS):
        bm_g = bm[:, g * SSM_STATE:(g + 1) * SSM_STATE]
        cm_g = cm[:, g * SSM_STATE:(g + 1) * SSM_STATE]
        cm_b = cm_g.astype(BF16)
        cb = lax.dot_general(cm_b, bm_g.astype(BF16), (((1,), (1,)), ((), ())),
                             preferred_element_type=F32)
        parts = []
        for pp in range(hpg // 2):
            p = g * (hpg // 2) + pp
            xp = xdt_b[:, p * LANES:(p + 1) * LANES]
            ys = []
            for hh in range(2):
                h = 2 * p + hh
                seg = acs[:, h:h + 1] - acs_t[h:h + 1, :]
                lm = jnp.where(tril, jnp.exp(seg), 0.0)
                ys.append(jnp.dot((cb * lm).astype(BF16), xp, preferred_element_type=F32))
            parts.append(jnp.where(lane < SSM_HEAD_DIM, ys[0], ys[1]))
        y_diag = jnp.concatenate(parts, axis=1)
        st_g = st_ref[:, g * gw:(g + 1) * gw]
        y_off = jnp.dot(cm_b, st_g.astype(BF16), preferred_element_type=F32) * expa[:, g * gw:(g + 1) * gw]
        new = jnp.dot(bm_g.T.astype(BF16), xds_b[:, g * gw:(g + 1) * gw], preferred_element_type=F32)
        st_ref[:, g * gw:(g + 1) * gw] = st_g * cdec[:, g * gw:(g + 1) * gw] + new
        y_groups.append(y_diag + y_off)

    zz = z_ref[...]
    gate = zz * _sigmoid(zz)
    dsk = dsk_ref[...]
    ng = ng_ref[...]
    outs = []
    for g in range(SSM_GROUPS):
        sl = slice(g * gw, (g + 1) * gw)
        yg = (y_groups[g] + xs[:, sl] * dsk[:, sl]) * gate[:, sl]
        ms = jnp.mean(yg * yg, axis=-1, keepdims=True)
        outs.append(yg * lax.rsqrt(ms + NORM_EPS) * ng[:, sl])
    y_ref[...] = jnp.concatenate(outs, axis=1).astype(BF16)


def _ssd(proj, bsz, seq, cw, cb, dt_bias, a_log, d_skip, norm_g):
    t = bsz * seq
    nc = seq // CHUNK
    cwx, cwb, cwc = cw[:, :D_SSM], cw[:, D_SSM:D_SSM + BC_W], cw[:, D_SSM + BC_W:]
    cb = cb[None, :]
    cbx, cbb, cbc = cb[:, :D_SSM], cb[:, D_SSM:D_SSM + BC_W], cb[:, D_SSM + BC_W:]
    pad = LANES - SSM_HEADS
    dtb = jnp.pad(dt_bias, (0, pad))[None, :]
    alog = jnp.pad(a_log, (0, pad))[None, :]
    dsk = jnp.repeat(d_skip, SSM_HEAD_DIM)[None, :]
    expand = (jnp.arange(D_SSM)[None, :] // SSM_HEAD_DIM == jnp.arange(LANES)[:, None]).astype(F32)

    def row(b, c):
        return b * nc + c

    def full(shape):
        return pl.BlockSpec(shape, lambda b, c: (0, 0))

    bc0 = (2 * D_SSM + 2 * D_QK + D_V) // BC_W
    dt0 = (PROJ_COLS - LANES) // LANES
    return pl.pallas_call(
        _ssd_body,
        grid=(bsz, nc),
        in_specs=[pl.BlockSpec((CHUNK, D_SSM), lambda b, c: (row(b, c), 0)),
                  pl.BlockSpec((CHUNK, D_SSM), lambda b, c: (row(b, c), 1)),
                  pl.BlockSpec((CHUNK, BC_W), lambda b, c: (row(b, c), bc0)),
                  pl.BlockSpec((CHUNK, BC_W), lambda b, c: (row(b, c), bc0 + 1)),
                  pl.BlockSpec((CHUNK, LANES), lambda b, c: (row(b, c), dt0)),
                  full((CONV_WIDTH, D_SSM)), full((CONV_WIDTH, BC_W)), full((CONV_WIDTH, BC_W)),
                  full((1, D_SSM)), full((1, BC_W)), full((1, BC_W)),
                  full((1, LANES)), full((1, LANES)), full((1, D_SSM)), full((1, D_SSM)),
                  full((LANES, D_SSM))],
        out_specs=pl.BlockSpec((CHUNK, D_SSM), lambda b, c: (row(b, c), 0)),
        out_shape=jax.ShapeDtypeStruct((t, D_SSM), BF16),
        scratch_shapes=[pltpu.VMEM((SUBLANES, D_SSM), F32), pltpu.VMEM((SUBLANES, BC_W), F32),
                        pltpu.VMEM((SUBLANES, BC_W), F32), pltpu.VMEM((SSM_STATE, D_SSM), F32)],
        compiler_params=_cparams(2),
        name="ssd",
    )(proj, proj, proj, proj, proj, cwx, cwb, cwc, cbx, cbb, cbc, dtb, alog, dsk, norm_g[None, :], expand)


def _qkprep_body(q_ref, k_ref, v_ref, cos_ref, sin_ref, gq_ref, gk_ref, gm_ref, qo_ref, ko_ref, vo_ref):
    reps = D_QK // LANES
    cos = jnp.tile(cos_ref[...], (1, reps))
    sin = jnp.tile(sin_ref[...], (1, reps))
    lane = lax.broadcasted_iota(I32, (1, D_QK), 1)
    first = (lane & (DIFF_HEAD_DIM - 1)) < DIFF_HEAD_DIM // 2
    gm = gm_ref[...]

    def prep(t, g, scale):
        sq = t * t
        hi = sq.astype(BF16)
        lo = (sq - hi.astype(F32)).astype(BF16)
        ss = jnp.dot(hi, gm, preferred_element_type=F32) + jnp.dot(lo, gm, preferred_element_type=F32)
        tn = t * lax.rsqrt(ss * (1.0 / DIFF_HEAD_DIM) + NORM_EPS) * g
        half = DIFF_HEAD_DIM // 2
        rot = jnp.where(first, pltpu.roll(tn, D_QK - half, 1), pltpu.roll(tn, half, 1))
        return (tn * cos + rot * sin) * scale

    qo_ref[...] = prep(q_ref[...], gq_ref[...], DIFF_HEAD_DIM ** -0.5 * math.log2(math.e)).astype(BF16)
    ko_ref[...] = prep(k_ref[...], gk_ref[...], 1.0).astype(BF16)
    vo_ref[...] = v_ref[...].astype(BF16)


def _qk_prep(proj, cos_t, sin_t, gq, gk):
    t = proj.shape[0]
    tm = 256
    gsum = (jnp.arange(D_QK)[:, None] // DIFF_HEAD_DIM == jnp.arange(D_QK)[None, :] // DIFF_HEAD_DIM).astype(BF16)
    out = jax.ShapeDtypeStruct((t, D_QK), BF16)
    return pl.pallas_call(
        _qkprep_body,
        grid=(t // tm,),
        in_specs=[pl.BlockSpec((tm, D_QK), lambda i: (i, 2)),
                  pl.BlockSpec((tm, D_QK), lambda i: (i, 3)),
                  pl.BlockSpec((tm, D_V), lambda i: (i, 4)),
                  pl.BlockSpec((tm, LANES), lambda i: (i, 0)),
                  pl.BlockSpec((tm, LANES), lambda i: (i, 0)),
                  pl.BlockSpec((1, D_QK), lambda i: (0, 0)),
                  pl.BlockSpec((1, D_QK), lambda i: (0, 0)),
                  pl.BlockSpec((D_QK, D_QK), lambda i: (0, 0))],
        out_specs=[pl.BlockSpec((tm, D_QK), lambda i: (i, 0))] * 3,
        out_shape=[out, out, out],
        compiler_params=_cparams(1),
        name="qk_prep",
    )(proj, proj, proj, cos_t, sin_t, gq, gk, gsum)


KV_UNROLL = 4


def _attn_body(lam_ref, q_ref, k_ref, v_ref, sg_ref, o_ref, qs_ref, vx_ref, m_ref, acc_ref, *, tq, row_chunk,
               out_scale):
    i = pl.program_id(2)

    @pl.when(i == 0)
    def _():
        vx_ref[:, 0:LANES] = v_ref[...]
        vx_ref[:, LANES:2 * LANES] = jnp.ones((vx_ref.shape[0], LANES), BF16)

    q = q_ref[...]
    lane = lax.broadcasted_iota(I32, (1, LANES), 1)
    zero = jnp.zeros_like(q)
    qs_ref[0:tq, :] = jnp.where(lane < DIFF_HEAD_DIM, q, zero)
    qs_ref[tq:2 * tq, :] = jnp.where(lane >= DIFF_HEAD_DIM, q, zero)
    m_ref[...] = jnp.full_like(m_ref, -jnp.inf)
    acc_ref[...] = jnp.zeros_like(acc_ref)

    def step(j, masked):
        start = pl.multiple_of(j * tq, tq)
        k = k_ref[pl.ds(start, tq), :]
        vx = vx_ref[pl.ds(start, tq), :]
        for rc in range(2 * tq // row_chunk):
            rows = slice(rc * row_chunk, (rc + 1) * row_chunk)
            s = lax.dot_general(qs_ref[rows, :], k, (((1,), (1,)), ((), ())), preferred_element_type=F32)
            if masked:
                r = lax.broadcasted_iota(I32, (row_chunk, tq), 0) + (rc * row_chunk) % tq
                c = lax.broadcasted_iota(I32, (row_chunk, tq), 1)
                s = jnp.where(c <= r, s, -jnp.inf)
            m_prev = m_ref[rows, :]
            m_new = jnp.maximum(m_prev, jnp.max(s, axis=1, keepdims=True))
            alpha = jnp.exp2(m_prev - m_new)
            p = jnp.exp2(s - jnp.tile(m_new, (1, tq // LANES)))
            pv = jnp.dot(p.astype(BF16), vx, preferred_element_type=F32)
            acc_ref[rows, :] = jnp.tile(alpha, (1, 2)) * acc_ref[rows, :] + pv
            m_ref[rows, :] = m_new

    def off_diagonal_group(jj, carry):
        for u in range(KV_UNROLL):
            step(KV_UNROLL * jj + u, False)
        return carry

    lax.fori_loop(0, i // KV_UNROLL, off_diagonal_group, 0)
    done = (i // KV_UNROLL) * KV_UNROLL
    width = KV_UNROLL // 2
    while width >= 1:
        @pl.when((i & width) != 0)
        def _(done=done, width=width):
            for u in range(width):
                step(done + u, False)
        done = done + (i & width)
        width //= 2

    step(i, True)

    acc = acc_ref[...]
    o = acc[:, 0:LANES] / acc[:, LANES:2 * LANES]
    d = o[0:tq, :] - lam_ref[0] * o[tq:2 * tq, :]
    ms = jnp.mean(d * d, axis=-1, keepdims=True)
    o_ref[...] = (d * lax.rsqrt(ms + NORM_EPS) * sg_ref[...] * out_scale).astype(BF16)


def _attention(qr, kr, vb, lam, subln_g, bsz, seq, out_scale):
    t = bsz * seq
    tq = 512
    nq = seq // tq
    body = functools.partial(_attn_body, tq=tq, row_chunk=128, out_scale=out_scale)
    grid_spec = pltpu.PrefetchScalarGridSpec(
        num_scalar_prefetch=1,
        grid=(bsz, DIFF_HEADS, nq),
        in_specs=[pl.BlockSpec((tq, LANES), lambda b, h, i, lam: (b * nq + i, h)),
                  pl.BlockSpec((seq, LANES), lambda b, h, i, lam: (b, h)),
                  pl.BlockSpec((seq, LANES), lambda b, h, i, lam: (b, h)),
                  pl.BlockSpec((1, LANES), lambda b, h, i, lam: (0, 0))],
        out_specs=pl.BlockSpec((tq, LANES), lambda b, h, i, lam: (b * nq + i, h)),
        scratch_shapes=[pltpu.VMEM((2 * tq, LANES), BF16), pltpu.VMEM((seq, 2 * LANES), BF16),
                        pltpu.VMEM((2 * tq, LANES), F32), pltpu.VMEM((2 * tq, 2 * LANES), F32)],
    )
    return pl.pallas_call(
        body, grid_spec=grid_spec,
        out_shape=jax.ShapeDtypeStruct((t, D_ATTN), BF16),
        compiler_params=_cparams(3),
        name="diff_attention",
    )(lam, qr, kr, vb, subln_g[None, :])


def _outproj_body(x_ref, ys_ref, ya_ref, wt_ref, wb_ref, g2_ref, wrh_ref, wrl_ref, br_ref, ltri_ref,
                  h1_ref, hp_ref, ti_ref, tg_ref, cnt_out_ref, cnt_ref):
    h1 = (x_ref[...] + jnp.dot(ys_ref[...], wt_ref[...], preferred_element_type=F32)
          + jnp.dot(ya_ref[...], wb_ref[...], preferred_element_type=F32))
    h1_ref[...] = h1
    ms = jnp.mean(h1 * h1, axis=-1, keepdims=True)
    h2 = h1 * lax.rsqrt(ms + NORM_EPS) * g2_ref[...]
    hp_ref[...] = _pack2(h2[:, :HALF], h2[:, HALF:])

    h2_hi = h2.astype(BF16)
    h2_lo = (h2 - h2_hi.astype(F32)).astype(BF16)
    logits = (jnp.dot(h2_hi, wrh_ref[...], preferred_element_type=F32)
              + jnp.dot(h2_hi, wrl_ref[...], preferred_element_type=F32)
              + jnp.dot(h2_lo, wrh_ref[...], preferred_element_type=F32) + br_ref[...])
    lane = lax.broadcasted_iota(I32, (1, LANES), 1)
    lanef = lane.astype(F32)
    cur = jnp.where(lane < N_EXPERTS, logits, -jnp.inf)
    vals, idxs = [], []
    for _ in range(TOP_K):
        m = jnp.max(cur, axis=1, keepdims=True)
        idx = jnp.min(jnp.where(cur == m, lanef, float(LANES)), axis=1, keepdims=True)
        vals.append(m)
        idxs.append(idx)
        cur = jnp.where(lanef == idx, -jnp.inf, cur)
    es = [jnp.exp(v - vals[0]) for v in vals]
    den = es[0] + es[1] + es[2] + es[3]

    @pl.when(pl.program_id(0) == 0)
    def _():
        cnt_ref[...] = jnp.zeros_like(cnt_ref)

    base = cnt_ref[...]
    ltri = ltri_ref[...]
    ti = jnp.zeros(logits.shape, F32)
    tg = jnp.zeros(logits.shape, F32)
    for kk in range(TOP_K):
        oh = lanef == idxs[kk]
        ohf = jnp.where(oh, 1.0, 0.0)
        pre = jnp.dot(ltri, ohf.astype(BF16), preferred_element_type=F32)
        rank = jnp.sum(jnp.where(oh, pre + base, 0.0), axis=1, keepdims=True)
        base = base + jnp.sum(ohf, axis=0, keepdims=True)
        ti = jnp.where(lane == kk, idxs[kk], ti)
        ti = jnp.where(lane == TOP_K + kk, rank, ti)
        tg = jnp.where(lane == kk, es[kk] / den, tg)
    cnt_ref[...] = base
    ti_ref[...] = ti.astype(I32)
    tg_ref[...] = tg
    cnt_out_ref[...] = jnp.broadcast_to(base, cnt_out_ref.shape).astype(I32)


def _out_proj_router(x2, y_ssm, y_attn, w_top, w_bot, g2, w_r, b_r):
    t, d = x2.shape
    tm = 256
    w_rh = w_r.astype(BF16)
    w_rl = (w_r - w_rh.astype(F32)).astype(BF16)
    ltri = (jnp.arange(tm)[:, None] > jnp.arange(tm)[None, :]).astype(BF16)

    def full(shape):
        return pl.BlockSpec(shape, lambda i: (0, 0))

    return pl.pallas_call(
        _outproj_body,
        grid=(t // tm,),
        in_specs=[pl.BlockSpec((tm, d), lambda i: (i, 0)),
                  pl.BlockSpec((tm, D_SSM), lambda i: (i, 0)),
                  pl.BlockSpec((tm, D_ATTN), lambda i: (i, 0)),
                  full((D_SSM, d)), full((D_ATTN, d)), full((1, d)), full((d, LANES)), full((d, LANES)),
                  full((1, LANES)), full((tm, tm))],
        out_specs=[pl.BlockSpec((tm, d), lambda i: (i, 0)),
                   pl.BlockSpec((tm, HALF), lambda i: (i, 0)),
                   pl.BlockSpec((tm, LANES), lambda i: (i, 0)),
                   pl.BlockSpec((tm, LANES), lambda i: (i, 0)),
                   full((SUBLANES, LANES))],
        out_shape=[jax.ShapeDtypeStruct((t, d), F32), jax.ShapeDtypeStruct((t, HALF), U32),
                   jax.ShapeDtypeStruct((t, LANES), I32), jax.ShapeDtypeStruct((t, LANES), F32),
                   jax.ShapeDtypeStruct((SUBLANES, LANES), I32)],
        scratch_shapes=[pltpu.VMEM((1, LANES), F32)],
        compiler_params=_cparams(1),
        name="out_proj_router",
    )(x2, y_ssm, y_attn, w_top, w_bot, g2, w_rh, w_rl, b_r, ltri)


def _row_copy(src_hbm, src_row, dst_ref, dst_row, sem):
    return pltpu.make_async_copy(src_hbm.at[pl.ds(src_row, 1), :], dst_ref.at[pl.ds(dst_row, 1), :], sem)


DMA_UNROLL = 8


def _gather_body(idx_ref, src_hbm, dst_hbm, sem, *, rows):
    i = pl.program_id(0)
    last = pl.num_programs(0) - 1

    def start(g, carry):
        for u in range(DMA_UNROLL):
            r = g * DMA_UNROLL + u
            _row_copy(src_hbm, idx_ref[0, 0, r], dst_hbm, i * rows + r, sem).start()
        return carry

    def wait_step(step):
        def wait(g, carry):
            for u in range(DMA_UNROLL):
                r = g * DMA_UNROLL + u
                _row_copy(src_hbm, 0, dst_hbm, step * rows + r, sem).wait()
            return carry
        lax.fori_loop(0, rows // DMA_UNROLL, wait, 0)

    lax.fori_loop(0, rows // DMA_UNROLL, start, 0)

    @pl.when(i > 0)
    def _():
        wait_step(i - 1)

    @pl.when(i == last)
    def _():
        wait_step(i)


def _gather_rows(src, row_ids, rows_per_step):
    r = row_ids.shape[0]
    w = src.shape[1]
    nsteps = r // rows_per_step
    return pl.pallas_call(
        functools.partial(_gather_body, rows=rows_per_step),
        grid=(nsteps,),
        in_specs=[pl.BlockSpec((1, 1, rows_per_step), lambda i: (i, 0, 0), memory_space=pltpu.SMEM),
                  pl.BlockSpec(memory_space=pl.ANY)],
        out_specs=pl.BlockSpec(memory_space=pl.ANY),
        out_shape=jax.ShapeDtypeStruct((r, w), src.dtype),
        scratch_shapes=[pltpu.SemaphoreType.DMA(())],
        compiler_params=_cparams(1),
        name="gather_rows",
    )(row_ids.reshape(nsteps, 1, rows_per_step), src)


MXU_W = 256


def _w1split_body(w_ref, p_ref, g_ref, l_ref):
    w = w_ref[...].astype(BF16)
    half = MXU_W // 2
    for c in range(w.shape[1] // MXU_W):
        r = jnp.dot(w[:, c * MXU_W:(c + 1) * MXU_W], p_ref[...], preferred_element_type=F32)
        g_ref[:, c * half:(c + 1) * half] = r[:, :half].astype(BF16)
        l_ref[:, c * half:(c + 1) * half] = r[:, half:].astype(BF16)


def _w1_split(w1):
    e, d, c2 = w1.shape
    tr, tcw = 512, 2048
    j = jnp.arange(MXU_W)
    perm = (j[None, :] == jnp.where(j % 2 == 0, j // 2, MXU_W // 2 + j // 2)[:, None]).astype(BF16)
    out = jax.ShapeDtypeStruct((e, d, c2 // 2), BF16)
    return pl.pallas_call(
        _w1split_body,
        grid=(e, d // tr, c2 // tcw),
        in_specs=[pl.BlockSpec((None, tr, tcw), lambda a, b, c: (a, b, c)),
                  pl.BlockSpec((MXU_W, MXU_W), lambda a, b, c: (0, 0))],
        out_specs=[pl.BlockSpec((None, tr, tcw // 2), lambda a, b, c: (a, b, c))] * 2,
        out_shape=[out, out],
        compiler_params=_cparams(3),
        name="w1_split",
    )(w1, perm)


def _moe_up_body(te_ref, nv_ref, x_ref, wg_ref, wl_ref, bg_ref, bl_ref, o_ref):
    i = pl.program_id(1)

    @pl.when(i < nv_ref[0])
    def _():
        w = x_ref[...]
        lo = _unpack2(w, 0).astype(BF16)
        hi = _unpack2(w, 1).astype(BF16)

        def up(w_ref, b_ref):
            return (jnp.dot(lo, w_ref[0:HALF, :], preferred_element_type=F32)
                    + jnp.dot(hi, w_ref[HALF:, :], preferred_element_type=F32) + b_ref[...])

        glu = jnp.minimum(up(wg_ref, bg_ref), SWIGLU_LIMIT)
        lin = jnp.clip(up(wl_ref, bl_ref), -SWIGLU_LIMIT, SWIGLU_LIMIT)
        o_ref[...] = (glu * _sigmoid(SWIGLU_ALPHA * glu) * (lin + 1.0)).astype(BF16)

    @pl.when(i >= nv_ref[0])
    def _():
        o_ref[...] = jnp.zeros_like(o_ref)


def _moe_up(tile_expert, n_valid, xs, w1g, w1l, b1g, b1l):
    r = xs.shape[0]
    tm, tn = MOE_TM, 1024
    nt = r // tm
    grid_spec = pltpu.PrefetchScalarGridSpec(
        num_scalar_prefetch=2,
        grid=(D_EXPERT // tn, nt),
        in_specs=[pl.BlockSpec((tm, HALF), lambda n, i, te, nv: (i, 0)),
                  pl.BlockSpec((None, D_MODEL, tn), lambda n, i, te, nv: (te[i], 0, n)),
                  pl.BlockSpec((None, D_MODEL, tn), lambda n, i, te, nv: (te[i], 0, n)),
                  pl.BlockSpec((None, 1, tn), lambda n, i, te, nv: (te[i], 0, n)),
                  pl.BlockSpec((None, 1, tn), lambda n, i, te, nv: (te[i], 0, n))],
        out_specs=pl.BlockSpec((tm, tn), lambda n, i, te, nv: (i, n)),
    )
    return pl.pallas_call(
        _moe_up_body, grid_spec=grid_spec,
        out_shape=jax.ShapeDtypeStruct((r, D_EXPERT), BF16),
        compiler_params=_cparams(2),
        name="moe_up",
    )(tile_expert, n_valid, xs, w1g, w1l, b1g, b1l)


def _moe_down_body(te_ref, nv_ref, a_ref, w_ref, b_ref, o_ref):
    i = pl.program_id(0)

    @pl.when(i < nv_ref[0])
    def _():
        y = jnp.dot(a_ref[...], w_ref[...], preferred_element_type=F32) + b_ref[...]
        o_ref[...] = _pack2(y[:, :HALF], y[:, HALF:])

    @pl.when(i >= nv_ref[0])
    def _():
        zero = jnp.zeros(o_ref.shape, F32)
        o_ref[...] = _pack2(zero, zero)


def _moe_down(tile_expert, n_valid, act, w2b, b2):
    r = act.shape[0]
    tm = MOE_TM
    nt = r // tm
    grid_spec = pltpu.PrefetchScalarGridSpec(
        num_scalar_prefetch=2,
        grid=(nt,),
        in_specs=[pl.BlockSpec((tm, D_EXPERT), lambda i, te, nv: (i, 0)),
                  pl.BlockSpec((None, D_EXPERT, D_MODEL), lambda i, te, nv: (te[i], 0, 0)),
                  pl.BlockSpec((None, 1, D_MODEL), lambda i, te, nv: (te[i], 0, 0))],
        out_specs=pl.BlockSpec((tm, HALF), lambda i, te, nv: (i, 0)),
    )
    return pl.pallas_call(
        _moe_down_body, grid_spec=grid_spec,
        out_shape=jax.ShapeDtypeStruct((r, HALF), U32),
        compiler_params=_cparams(1),
        name="moe_down",
    )(tile_expert, n_valid, act, w2b, b2)


def _combine_body(idx_ref, idx_next_ref, y_hbm, h1_ref, g_ref, o_ref, buf_ref, sem, *, tc):
    n = TOP_K * tc
    i = pl.program_id(0)
    slot = i % 2

    def issue(ids_ref, s):
        def start(g, carry):
            for u in range(DMA_UNROLL):
                a = g * DMA_UNROLL + u
                _row_copy(y_hbm, ids_ref[0, 0, a], buf_ref.at[s], a, sem.at[s]).start(priority=u % 2)
            return carry
        lax.fori_loop(0, n // DMA_UNROLL, start, 0)

    @pl.when(i == 0)
    def _():
        issue(idx_ref, 0)

    @pl.when(i + 1 < pl.num_programs(0))
    def _():
        issue(idx_next_ref, 1 - slot)

    def wait(g, carry):
        for u in range(DMA_UNROLL):
            a = g * DMA_UNROLL + u
            _row_copy(y_hbm, 0, buf_ref.at[slot], a, sem.at[slot]).wait()
        return carry

    lax.fori_loop(0, n // DMA_UNROLL, wait, 0)

    g = g_ref[...]
    lo = h1_ref[:, :HALF]
    hi = h1_ref[:, HALF:]
    for kk in range(TOP_K):
        w = buf_ref[slot, kk * tc:(kk + 1) * tc, :]
        gk = g[:, kk:kk + 1]
        lo = lo + gk * _unpack2(w, 0)
        hi = hi + gk * _unpack2(w, 1)
    o_ref[:, :HALF] = lo
    o_ref[:, HALF:] = hi


def _combine(pos_km, y_rows, h1, gates):
    t, d = h1.shape
    tc = 128
    nsteps = t // tc
    return pl.pallas_call(
        functools.partial(_combine_body, tc=tc),
        grid=(nsteps,),
        in_specs=[pl.BlockSpec((1, 1, TOP_K * tc), lambda i: (i, 0, 0), memory_space=pltpu.SMEM),
                  pl.BlockSpec((1, 1, TOP_K * tc), lambda i: (jnp.minimum(i + 1, nsteps - 1), 0, 0),
                               memory_space=pltpu.SMEM),
                  pl.BlockSpec(memory_space=pl.ANY),
                  pl.BlockSpec((tc, d), lambda i: (i, 0)),
                  pl.BlockSpec((tc, LANES), lambda i: (i, 0))],
        out_specs=pl.BlockSpec((tc, d), lambda i: (i, 0)),
        out_shape=jax.ShapeDtypeStruct((t, d), F32),
        scratch_shapes=[pltpu.VMEM((2, TOP_K * tc, HALF), U32), pltpu.SemaphoreType.DMA((2,))],
        compiler_params=_cparams(1),
        name="moe_combine",
    )(pos_km, pos_km, y_rows, h1, gates)


def _route(ti, counts, tm):
    t = ti.shape[0]
    n_assign = t * TOP_K
    n_rows = n_assign + N_EXPERTS * tm
    nt = n_rows // tm
    top_e = ti[:, :TOP_K]
    rank = ti[:, TOP_K:2 * TOP_K]
    counts = counts[0, :N_EXPERTS]
    tiles_per = (counts + (tm - 1)) // tm
    tile_end = jnp.cumsum(tiles_per)
    pstart = (tile_end - tiles_per) * tm
    experts = jnp.arange(N_EXPERTS, dtype=I32)
    onehot = top_e[:, :, None] == experts[None, None, :]
    pos = jnp.sum(jnp.where(onehot, pstart[None, None, :], 0), axis=-1) + rank
    token_of = jnp.arange(n_assign, dtype=I32) // TOP_K
    row_token = jnp.zeros((n_rows,), I32).at[pos.reshape(-1)].set(token_of, mode="drop", unique_indices=True)
    n_valid = tile_end[N_EXPERTS - 1]
    tile_ids = jnp.minimum(jnp.arange(nt, dtype=I32), n_valid - 1)
    tile_expert = jnp.sum((tile_end[None, :] <= tile_ids[:, None]).astype(I32), axis=1)
    tile_expert = jnp.minimum(tile_expert, N_EXPERTS - 1)
    return row_token, pos, tile_expert, n_valid.reshape(1)


def _layer(h, cos_t, sin_t, lam_init, norm1_g, w_in, conv_w, conv_b, dt_bias, a_log, d_skip, ssm_norm_g,
           q_norm_g, k_norm_g, lq1, lk1, lq2, lk2, subln_g, w_out, norm2_g, w_router, b_router, w1, b1, w2, b2):
    bsz, seq, d = h.shape
    t = bsz * seq
    x2 = h.reshape(t, d)

    o_z = 0
    o_xbc = D_SSM
    o_dt = o_xbc + D_SSM + 2 * BC_W
    o_q = o_dt + SSM_HEADS
    o_k = o_q + D_QK
    o_v = o_k + D_QK
    w_p = jnp.concatenate([
        w_in[:, o_z:o_z + D_SSM], w_in[:, o_xbc:o_xbc + D_SSM],
        w_in[:, o_q:o_q + D_QK], w_in[:, o_k:o_k + D_QK], w_in[:, o_v:o_v + D_V],
        w_in[:, o_xbc + D_SSM:o_xbc + D_SSM + 2 * BC_W],
        jnp.pad(w_in[:, o_dt:o_dt + SSM_HEADS], ((0, 0), (0, LANES - SSM_HEADS)))], axis=1).astype(BF16)

    proj = _in_proj(x2, norm1_g[None, :], w_p)
    y_ssm = _ssd(proj, bsz, seq, conv_w, conv_b, dt_bias, a_log, d_skip, ssm_norm_g)

    reps = D_QK // DIFF_HEAD_DIM
    qr, kr, vb = _qk_prep(proj, cos_t, sin_t, jnp.tile(q_norm_g, reps)[None, :], jnp.tile(k_norm_g, reps)[None, :])
    lam = (jnp.exp(jnp.sum(lq1 * lk1)) - jnp.exp(jnp.sum(lq2 * lk2)) + lam_init).reshape(1).astype(F32)
    y_attn = _attention(qr, kr, vb, lam, subln_g, bsz, seq, 1.0 - lam_init)

    w_ob = w_out.astype(BF16)
    w_r = jnp.pad(w_router, ((0, 0), (0, LANES - N_EXPERTS)))
    b_r = jnp.pad(b_router, (0, LANES - N_EXPERTS))[None, :]
    h1, h2p, ti, tg, counts = _out_proj_router(x2, y_ssm, y_attn, w_ob[:D_SSM], w_ob[D_SSM:], norm2_g[None, :],
                                               w_r, b_r)

    row_token, pos, tile_expert, n_valid = _route(ti, counts, MOE_TM)
    xs = _gather_rows(h2p, row_token, MOE_TM)
    w1g, w1l = _w1_split(w1)
    act = _moe_up(tile_expert, n_valid, xs, w1g, w1l, b1[:, None, 0::2], b1[:, None, 1::2])
    y_rows = _moe_down(tile_expert, n_valid, act, w2.astype(BF16), b2[:, None, :])

    tc = 128
    pos_km = pos.reshape(t // tc, tc, TOP_K).transpose(0, 2, 1).reshape(t // tc, 1, TOP_K * tc)
    out = _combine(pos_km, y_rows, h1, tg)
    return out.reshape(bsz, seq, d)


def kernel(x, positions, norm1_g, w_in, conv_w, conv_b, dt_bias, a_log, d_skip, ssm_norm_g, q_norm_g, k_norm_g,
           lambda_q1, lambda_k1, lambda_q2, lambda_k2, subln_g, w_out, norm2_g, w_router, b_router, w1, b1, w2, b2):
    bsz, seq, _ = x.shape
    inv = ROPE_THETA ** (-jnp.arange(0, DIFF_HEAD_DIM, 2, dtype=F32) / DIFF_HEAD_DIM)
    ang = positions.astype(F32).reshape(bsz * seq, 1) * inv[None, :]
    cos, sin = jnp.cos(ang), jnp.sin(ang)
    cos_t = jnp.tile(cos, (1, 2 * LANES // DIFF_HEAD_DIM))
    sin_t = jnp.tile(jnp.concatenate([-sin, sin], axis=1), (1, LANES // DIFF_HEAD_DIM))

    h = x
    for layer in range(norm1_g.shape[0]):
        lam_init = 0.8 - 0.6 * math.exp(-0.3 * layer)
        h = _layer(h, cos_t, sin_t, lam_init, norm1_g[layer], w_in[layer], conv_w[layer], conv_b[layer],
                   dt_bias[layer], a_log[layer], d_skip[layer], ssm_norm_g[layer], q_norm_g[layer],
                   k_norm_g[layer], lambda_q1[layer], lambda_k1[layer], lambda_q2[layer], lambda_k2[layer],
                   subln_g[layer], w_out[layer], norm2_g[layer], w_router[layer], b_router[layer],
                   w1[layer], b1[layer], w2[layer], b2[layer])
    return h
```

```python
import functools
import math

import jax
import jax.numpy as jnp
from jax import lax
from jax.experimental import pallas as pl
from jax.experimental.pallas import tpu as pltpu

F32 = jnp.float32
BF16 = jnp.bfloat16
U32 = jnp.uint32
I32 = jnp.int32
HIGHEST = lax.Precision.HIGHEST

D_MODEL = 2048
D_SSM = 1024
D_ATTN = 1024
SSM_HEAD_DIM = 64
SSM_HEADS = 16
SSM_GROUPS = 2
SSM_STATE = 128
CONV_WIDTH = 4
CHUNK = 128
DIFF_HEAD_DIM = 64
DIFF_V_DIM = 128
DIFF_HEADS = 8
D_QK = 1024
D_V = 1024
ROPE_THETA = 10000.0
N_EXPERTS = 32
TOP_K = 4
D_EXPERT = 2048
SWIGLU_ALPHA = 1.702
SWIGLU_LIMIT = 7.0
NORM_EPS = 1e-5
LANES = 128
SUBLANES = 8
HALF = D_MODEL // 2

BC_W = SSM_GROUPS * SSM_STATE
PROJ_COLS = D_SSM + D_SSM + D_QK + D_QK + D_V + 2 * BC_W + LANES
INPROJ_TN = 1536
PROJ_PAD = -PROJ_COLS % INPROJ_TN

MOE_TM = 256
VMEM_LIMIT = 56 * 1024 * 1024


def _cparams(n_axes, vmem=VMEM_LIMIT):
    return pltpu.CompilerParams(dimension_semantics=("arbitrary",) * n_axes, vmem_limit_bytes=vmem)


def _sigmoid(v):
    return 1.0 / (1.0 + jnp.exp(-v))


def _pack2(lo, hi):
    return pltpu.pack_elementwise([lo, hi], packed_dtype=BF16)


def _unpack2(w, index):
    return pltpu.unpack_elementwise(w, index=index, packed_dtype=BF16, unpacked_dtype=F32)


def _inproj_body(x_ref, g_ref, w_ref, o_ref, u_ref):
    @pl.when(pl.program_id(1) == 0)
    def _():
        x = x_ref[...]
        ms = jnp.mean(x * x, axis=-1, keepdims=True)
        u_ref[...] = (x * lax.rsqrt(ms + NORM_EPS) * g_ref[...]).astype(BF16)

    o_ref[...] = jnp.dot(u_ref[...], w_ref[...], preferred_element_type=F32)


def _in_proj(x2, g, w_p):
    t, d = x2.shape
    ncol = w_p.shape[1]
    tm, tn = 512, INPROJ_TN
    return pl.pallas_call(
        _inproj_body,
        grid=(t // tm, ncol // tn),
        in_specs=[pl.BlockSpec((tm, d), lambda i, j: (i, 0)),
                  pl.BlockSpec((1, d), lambda i, j: (0, 0)),
                  pl.BlockSpec((d, tn), lambda i, j: (0, j))],
        out_specs=pl.BlockSpec((tm, tn), lambda i, j: (i, j)),
        out_shape=jax.ShapeDtypeStruct((t, ncol), F32),
        scratch_shapes=[pltpu.VMEM((tm, d), BF16)],
        compiler_params=_cparams(2),
        name="in_proj",
    )(x2, g, w_p)


def _ssd_body(z_ref, x_ref, b_ref, c_ref, dt_ref, cwx_ref, cwb_ref, cwc_ref, cbx_ref, cbb_ref, cbc_ref,
              dtb_ref, alog_ref, dsk_ref, ng_ref, e_ref, y_ref, tx_ref, tb_ref, tc_ref, st_ref):
    @pl.when(pl.program_id(1) == 0)
    def _():
        tx_ref[...] = jnp.zeros_like(tx_ref)
        tb_ref[...] = jnp.zeros_like(tb_ref)
        tc_ref[...] = jnp.zeros_like(tc_ref)
        st_ref[...] = jnp.zeros_like(st_ref)

    row8 = lax.broadcasted_iota(I32, (SUBLANES, 1), 0)

    def conv_silu(u_ref, t_ref, w_ref, bias_ref):
        u = u_ref[...]
        tail = t_ref[...]
        w = w_ref[...]
        acc = u * w[CONV_WIDTH - 1:CONV_WIDTH, :] + bias_ref[...]
        for s in range(1, CONV_WIDTH):
            ru = pltpu.roll(u, s, 0)
            rt = pltpu.roll(tail, s, 0)
            head = jnp.where(row8 < s, rt, ru[0:SUBLANES, :])
            sh = jnp.concatenate([head, ru[SUBLANES:, :]], axis=0)
            acc = acc + sh * w[CONV_WIDTH - 1 - s:CONV_WIDTH - s, :]
        t_ref[...] = u[CHUNK - SUBLANES:CHUNK, :]
        return acc * _sigmoid(acc)

    xs = conv_silu(x_ref, tx_ref, cwx_ref, cbx_ref)
    bm = conv_silu(b_ref, tb_ref, cwb_ref, cbb_ref)
    cm = conv_silu(c_ref, tc_ref, cwc_ref, cbc_ref)

    dtr = dt_ref[...] + dtb_ref[...]
    dt = jnp.maximum(dtr, 0.0) + jnp.log1p(jnp.exp(-jnp.abs(dtr)))
    da = dt * (-jnp.exp(alog_ref[...]))

    rowi = lax.broadcasted_iota(I32, (CHUNK, 1), 0)
    acs = da
    k = 1
    while k < CHUNK:
        acs = acs + jnp.where(rowi >= k, pltpu.roll(acs, k, 0), 0.0)
        k *= 2
    acs_t = acs.T

    ex = jnp.dot(jnp.concatenate([dt, acs], axis=0), e_ref[...], precision=HIGHEST,
                 preferred_element_type=F32)
    dt_f = ex[:CHUNK]
    a_f = ex[CHUNK:]
    a_last = a_f[CHUNK - 1:CHUNK, :]
    expa = jnp.exp(a_f)
    dstate = jnp.exp(a_last - a_f)
    cdec = jnp.exp(a_last)

    xdt = xs * dt_f
    xdt_b = xdt.astype(BF16)
    xds_b = (xdt * dstate).astype(BF16)

    coli = lax.broadcasted_iota(I32, (1, CHUNK), 1)
    tril = rowi >= coli
    lane = lax.broadcasted_iota(I32, (1, LANES), 1)
    hpg = SSM_HEADS // SSM_GROUPS
    gw = hpg * SSM_HEAD_DIM
    y_groups = []
    for g in range(SSM_GROUPS):
        bm_g = bm[:, g * SSM_STATE:(g + 1) * SSM_STATE]
        cm_g = cm[:, g * SSM_STATE:(g + 1) * SSM_STATE]
        cm_b = cm_g.astype(BF16)
        cb = lax.dot_general(cm_b, bm_g.astype(BF16), (((1,), (1,)), ((), ())),
                             preferred_element_type=F32)
        parts = []
        for pp in range(hpg // 2):
            p = g * (hpg // 2) + pp
            xp = xdt_b[:, p * LANES:(p + 1) * LANES]
            ys = []
            for hh in range(2):
                h = 2 * p + hh
                seg = acs[:, h:h + 1] - acs_t[h:h + 1, :]
                lm = jnp.where(tril, jnp.exp(seg), 0.0)
                ys.append(jnp.dot((cb * lm).astype(BF16), xp, preferred_element_type=F32))
            parts.append(jnp.where(lane < SSM_HEAD_DIM, ys[0], ys[1]))
        y_diag = jnp.concatenate(parts, axis=1)
        st_g = st_ref[:, g * gw:(g + 1) * gw]
        y_off = jnp.dot(cm_b, st_g.astype(BF16), preferred_element_type=F32) * expa[:, g * gw:(g + 1) * gw]
        new = jnp.dot(bm_g.T.astype(BF16), xds_b[:, g * gw:(g + 1) * gw], preferred_element_type=F32)
        st_ref[:, g * gw:(g + 1) * gw] = st_g * cdec[:, g * gw:(g + 1) * gw] + new
        y_groups.append(y_diag + y_off)

    zz = z_ref[...]
    gate = zz * _sigmoid(zz)
    dsk = dsk_ref[...]
    ng = ng_ref[...]
    outs = []
    for g in range(SSM_GROUPS):
        sl = slice(g * gw, (g + 1) * gw)
        yg = (y_groups[g] + xs[:, sl] * dsk[:, sl]) * gate[:, sl]
        ms = jnp.mean(yg * yg, axis=-1, keepdims=True)
        outs.append(yg * lax.rsqrt(ms + NORM_EPS) * ng[:, sl])
    y_ref[...] = jnp.concatenate(outs, axis=1).astype(BF16)


def _ssd(proj, bsz, seq, cw, cb, dt_bias, a_log, d_skip, norm_g):
    t = bsz * seq
    nc = seq // CHUNK
    cwx, cwb, cwc = cw[:, :D_SSM], cw[:, D_SSM:D_SSM + BC_W], cw[:, D_SSM + BC_W:]
    cb = cb[None, :]
    cbx, cbb, cbc = cb[:, :D_SSM], cb[:, D_SSM:D_SSM + BC_W], cb[:, D_SSM + BC_W:]
    pad = LANES - SSM_HEADS
    dtb = jnp.pad(dt_bias, (0, pad))[None, :]
    alog = jnp.pad(a_log, (0, pad))[None, :]
    dsk = jnp.repeat(d_skip, SSM_HEAD_DIM)[None, :]
    expand = (jnp.arange(D_SSM)[None, :] // SSM_HEAD_DIM == jnp.arange(LANES)[:, None]).astype(F32)

    def row(b, c):
        return b * nc + c

    def full(shape):
        return pl.BlockSpec(shape, lambda b, c: (0, 0))

    bc0 = (2 * D_SSM + 2 * D_QK + D_V) // BC_W
    dt0 = (PROJ_COLS - LANES) // LANES
    return pl.pallas_call(
        _ssd_body,
        grid=(bsz, nc),
        in_specs=[pl.BlockSpec((CHUNK, D_SSM), lambda b, c: (row(b, c), 0)),
                  pl.BlockSpec((CHUNK, D_SSM), lambda b, c: (row(b, c), 1)),
                  pl.BlockSpec((CHUNK, BC_W), lambda b, c: (row(b, c), bc0)),
                  pl.BlockSpec((CHUNK, BC_W), lambda b, c: (row(b, c), bc0 + 1)),
                  pl.BlockSpec((CHUNK, LANES), lambda b, c: (row(b, c), dt0)),
                  full((CONV_WIDTH, D_SSM)), full((CONV_WIDTH, BC_W)), full((CONV_WIDTH, BC_W)),
                  full((1, D_SSM)), full((1, BC_W)), full((1, BC_W)),
                  full((1, LANES)), full((1, LANES)), full((1, D_SSM)), full((1, D_SSM)),
                  full((LANES, D_SSM))],
        out_specs=pl.BlockSpec((CHUNK, D_SSM), lambda b, c: (row(b, c), 0)),
        out_shape=jax.ShapeDtypeStruct((t, D_SSM), BF16),
        scratch_shapes=[pltpu.VMEM((SUBLANES, D_SSM), F32), pltpu.VMEM((SUBLANES, BC_W), F32),
                        pltpu.VMEM((SUBLANES, BC_W), F32), pltpu.VMEM((SSM_STATE, D_SSM), F32)],
        compiler_params=_cparams(2),
        name="ssd",
    )(proj, proj, proj, proj, proj, cwx, cwb, cwc, cbx, cbb, cbc, dtb, alog, dsk, norm_g[None, :], expand)


def _qkprep_body(q_ref, k_ref, v_ref, cos_ref, sin_ref, gq_ref, gk_ref, gm_ref, qo_ref, ko_ref, vo_ref):
    reps = D_QK // LANES
    cos = jnp.tile(cos_ref[...], (1, reps))
    sin = jnp.tile(sin_ref[...], (1, reps))
    lane = lax.broadcasted_iota(I32, (1, D_QK), 1)
    first = (lane & (DIFF_HEAD_DIM - 1)) < DIFF_HEAD_DIM // 2
    gm = gm_ref[...]

    def prep(t, g, scale):
        sq = t * t
        hi = sq.astype(BF16)
        lo = (sq - hi.astype(F32)).astype(BF16)
        ss = jnp.dot(hi, gm, preferred_element_type=F32) + jnp.dot(lo, gm, preferred_element_type=F32)
        tn = t * lax.rsqrt(ss * (1.0 / DIFF_HEAD_DIM) + NORM_EPS) * g
        half = DIFF_HEAD_DIM // 2
        rot = jnp.where(first, pltpu.roll(tn, D_QK - half, 1), pltpu.roll(tn, half, 1))
        return (tn * cos + rot * sin) * scale

    qo_ref[...] = prep(q_ref[...], gq_ref[...], DIFF_HEAD_DIM ** -0.5 * math.log2(math.e)).astype(BF16)
    ko_ref[...] = prep(k_ref[...], gk_ref[...], 1.0).astype(BF16)
    vo_ref[...] = v_ref[...].astype(BF16)


def _qk_prep(proj, cos_t, sin_t, gq, gk):
    t = proj.shape[0]
    tm = 256
    gsum = (jnp.arange(D_QK)[:, None] // DIFF_HEAD_DIM == jnp.arange(D_QK)[None, :] // DIFF_HEAD_DIM).astype(BF16)
    out = jax.ShapeDtypeStruct((t, D_QK), BF16)
    return pl.pallas_call(
        _qkprep_body,
        grid=(t // tm,),
        in_specs=[pl.BlockSpec((tm, D_QK), lambda i: (i, 2)),
                  pl.BlockSpec((tm, D_QK), lambda i: (i, 3)),
                  pl.BlockSpec((tm, D_V), lambda i: (i, 4)),
                  pl.BlockSpec((tm, LANES), lambda i: (i, 0)),
                  pl.BlockSpec((tm, LANES), lambda i: (i, 0)),
                  pl.BlockSpec((1, D_QK), lambda i: (0, 0)),
                  pl.BlockSpec((1, D_QK), lambda i: (0, 0)),
                  pl.BlockSpec((D_QK, D_QK), lambda i: (0, 0))],
        out_specs=[pl.BlockSpec((tm, D_QK), lambda i: (i, 0))] * 3,
        out_shape=[out, out, out],
        compiler_params=_cparams(1),
        name="qk_prep",
    )(proj, proj, proj, cos_t, sin_t, gq, gk, gsum)


KV_UNROLL = 4


def _attn_body(lam_ref, q_ref, k_ref, v_ref, sg_ref, o_ref, qs_ref, vx_ref, m_ref, acc_ref, *, tq, row_chunk,
               out_scale):
    i = pl.program_id(2)

    @pl.when(i == 0)
    def _():
        vx_ref[:, 0:LANES] = v_ref[...]
        vx_ref[:, LANES:2 * LANES] = jnp.ones((vx_ref.shape[0], LANES), BF16)

    q = q_ref[...]
    lane = lax.broadcasted_iota(I32, (1, LANES), 1)
    zero = jnp.zeros_like(q)
    qs_ref[0:tq, :] = jnp.where(lane < DIFF_HEAD_DIM, q, zero)
    qs_ref[tq:2 * tq, :] = jnp.where(lane >= DIFF_HEAD_DIM, q, zero)
    m_ref[...] = jnp.full_like(m_ref, -jnp.inf)
    acc_ref[...] = jnp.zeros_like(acc_ref)

    def step(j, masked):
        start = pl.multiple_of(j * tq, tq)
        k = k_ref[pl.ds(start, tq), :]
        vx = vx_ref[pl.ds(start, tq), :]
        for rc in range(2 * tq // row_chunk):
            rows = slice(rc * row_chunk, (rc + 1) * row_chunk)
            s = lax.dot_general(qs_ref[rows, :], k, (((1,), (1,)), ((), ())), preferred_element_type=F32)
            if masked:
                r = lax.broadcasted_iota(I32, (row_chunk, tq), 0) + (rc * row_chunk) % tq
                c = lax.broadcasted_iota(I32, (row_chunk, tq), 1)
                s = jnp.where(c <= r, s, -jnp.inf)
            m_prev = m_ref[rows, :]
            m_new = jnp.maximum(m_prev, jnp.max(s, axis=1, keepdims=True))
            alpha = jnp.exp2(m_prev - m_new)
            p = jnp.exp2(s - jnp.tile(m_new, (1, tq // LANES)))
            pv = jnp.dot(p.astype(BF16), vx, preferred_element_type=F32)
            acc_ref[rows, :] = jnp.tile(alpha, (1, 2)) * acc_ref[rows, :] + pv
            m_ref[rows, :] = m_new

    def off_diagonal_group(jj, carry):
        for u in range(KV_UNROLL):
            step(KV_UNROLL * jj + u, False)
        return carry

    lax.fori_loop(0, i // KV_UNROLL, off_diagonal_group, 0)
    done = (i // KV_UNROLL) * KV_UNROLL
    width = KV_UNROLL // 2
    while width >= 1:
        @pl.when((i & width) != 0)
        def _(done=done, width=width):
            for u in range(width):
                step(done + u, False)
        done = done + (i & width)
        width //= 2

    step(i, True)

    acc = acc_ref[...]
    o = acc[:, 0:LANES] / acc[:, LANES:2 * LANES]
    d = o[0:tq, :] - lam_ref[0] * o[tq:2 * tq, :]
    ms = jnp.mean(d * d, axis=-1, keepdims=True)
    o_ref[...] = (d * lax.rsqrt(ms + NORM_EPS) * sg_ref[...] * out_scale).astype(BF16)


def _attention(qr, kr, vb, lam, subln_g, bsz, seq, out_scale):
    t = bsz * seq
    tq = 512
    nq = seq // tq
    body = functools.partial(_attn_body, tq=tq, row_chunk=128, out_scale=out_scale)
    grid_spec = pltpu.PrefetchScalarGridSpec(
        num_scalar_prefetch=1,
        grid=(bsz, DIFF_HEADS, nq),
        in_specs=[pl.BlockSpec((tq, LANES), lambda b, h, i, lam: (b * nq + i, h)),
                  pl.BlockSpec((seq, LANES), lambda b, h, i, lam: (b, h)),
                  pl.BlockSpec((seq, LANES), lambda b, h, i, lam: (b, h)),
                  pl.BlockSpec((1, LANES), lambda b, h, i, lam: (0, 0))],
        out_specs=pl.BlockSpec((tq, LANES), lambda b, h, i, lam: (b * nq + i, h)),
        scratch_shapes=[pltpu.VMEM((2 * tq, LANES), BF16), pltpu.VMEM((seq, 2 * LANES), BF16),
                        pltpu.VMEM((2 * tq, LANES), F32), pltpu.VMEM((2 * tq, 2 * LANES), F32)],
    )
    return pl.pallas_call(
        body, grid_spec=grid_spec,
        out_shape=jax.ShapeDtypeStruct((t, D_ATTN), BF16),
        compiler_params=_cparams(3),
        name="diff_attention",
    )(lam, qr, kr, vb, subln_g[None, :])


def _outproj_body(x_ref, ys_ref, ya_ref, wt_ref, wb_ref, g2_ref, wrh_ref, wrl_ref, br_ref, ltri_ref,
                  h1_ref, hp_ref, ti_ref, tg_ref, cnt_out_ref, cnt_ref):
    h1 = (x_ref[...] + jnp.dot(ys_ref[...], wt_ref[...], preferred_element_type=F32)
          + jnp.dot(ya_ref[...], wb_ref[...], preferred_element_type=F32))
    h1_ref[...] = h1
    ms = jnp.mean(h1 * h1, axis=-1, keepdims=True)
    h2 = h1 * lax.rsqrt(ms + NORM_EPS) * g2_ref[...]
    hp_ref[...] = _pack2(h2[:, :HALF], h2[:, HALF:])

    h2_hi = h2.astype(BF16)
    h2_lo = (h2 - h2_hi.astype(F32)).astype(BF16)
    logits = (jnp.dot(h2_hi, wrh_ref[...], preferred_element_type=F32)
              + jnp.dot(h2_hi, wrl_ref[...], preferred_element_type=F32)
              + jnp.dot(h2_lo, wrh_ref[...], preferred_element_type=F32) + br_ref[...])
    lane = lax.broadcasted_iota(I32, (1, LANES), 1)
    lanef = lane.astype(F32)
    cur = jnp.where(lane < N_EXPERTS, logits, -jnp.inf)
    vals, idxs = [], []
    for _ in range(TOP_K):
        m = jnp.max(cur, axis=1, keepdims=True)
        idx = jnp.min(jnp.where(cur == m, lanef, float(LANES)), axis=1, keepdims=True)
        vals.append(m)
        idxs.append(idx)
        cur = jnp.where(lanef == idx, -jnp.inf, cur)
    es = [jnp.exp(v - vals[0]) for v in vals]
    den = es[0] + es[1] + es[2] + es[3]

    @pl.when(pl.program_id(0) == 0)
    def _():
        cnt_ref[...] = jnp.zeros_like(cnt_ref)

    base = cnt_ref[...]
    ltri = ltri_ref[...]
    ti = jnp.zeros(logits.shape, F32)
    tg = jnp.zeros(logits.shape, F32)
    for kk in range(TOP_K):
        oh = lanef == idxs[kk]
        ohf = jnp.where(oh, 1.0, 0.0)
        pre = jnp.dot(ltri, ohf.astype(BF16), preferred_element_type=F32)
        rank = jnp.sum(jnp.where(oh, pre + base, 0.0), axis=1, keepdims=True)
        base = base + jnp.sum(ohf, axis=0, keepdims=True)
        ti = jnp.where(lane == kk, idxs[kk], ti)
        ti = jnp.where(lane == TOP_K + kk, rank, ti)
        tg = jnp.where(lane == kk, es[kk] / den, tg)
    cnt_ref[...] = base
    ti_ref[...] = ti.astype(I32)
    tg_ref[...] = tg
    cnt_out_ref[...] = jnp.broadcast_to(base, cnt_out_ref.shape).astype(I32)


def _out_proj_router(x2, y_ssm, y_attn, w_top, w_bot, g2, w_r, b_r):
    t, d = x2.shape
    tm = 256
    w_rh = w_r.astype(BF16)
    w_rl = (w_r - w_rh.astype(F32)).astype(BF16)
    ltri = (jnp.arange(tm)[:, None] > jnp.arange(tm)[None, :]).astype(BF16)

    def full(shape):
        return pl.BlockSpec(shape, lambda i: (0, 0))

    return pl.pallas_call(
        _outproj_body,
        grid=(t // tm,),
        in_specs=[pl.BlockSpec((tm, d), lambda i: (i, 0)),
                  pl.BlockSpec((tm, D_SSM), lambda i: (i, 0)),
                  pl.BlockSpec((tm, D_ATTN), lambda i: (i, 0)),
                  full((D_SSM, d)), full((D_ATTN, d)), full((1, d)), full((d, LANES)), full((d, LANES)),
                  full((1, LANES)), full((tm, tm))],
        out_specs=[pl.BlockSpec((tm, d), lambda i: (i, 0)),
                   pl.BlockSpec((tm, HALF), lambda i: (i, 0)),
                   pl.BlockSpec((tm, LANES), lambda i: (i, 0)),
                   pl.BlockSpec((tm, LANES), lambda i: (i, 0)),
                   full((SUBLANES, LANES))],
        out_shape=[jax.ShapeDtypeStruct((t, d), F32), jax.ShapeDtypeStruct((t, HALF), U32),
                   jax.ShapeDtypeStruct((t, LANES), I32), jax.ShapeDtypeStruct((t, LANES), F32),
                   jax.ShapeDtypeStruct((SUBLANES, LANES), I32)],
        scratch_shapes=[pltpu.VMEM((1, LANES), F32)],
        compiler_params=_cparams(1),
        name="out_proj_router",
    )(x2, y_ssm, y_attn, w_top, w_bot, g2, w_rh, w_rl, b_r, ltri)


def _row_copy(src_hbm, src_row, dst_ref, dst_row, sem):
    return pltpu.make_async_copy(src_hbm.at[pl.ds(src_row, 1), :], dst_ref.at[pl.ds(dst_row, 1), :], sem)


DMA_UNROLL = 8


def _gather_body(idx_ref, idx_next_ref, src_hbm, o_ref, buf_ref, sem, *, rows):
    i = pl.program_id(0)
    slot = i % 2

    def issue(ids_ref, s):
        def start(g, carry):
            for u in range(DMA_UNROLL):
                r = g * DMA_UNROLL + u
                _row_copy(src_hbm, ids_ref[0, 0, r], buf_ref.at[s], r, sem.at[s]).start(priority=u % 2)
            return carry
        lax.fori_loop(0, rows // DMA_UNROLL, start, 0)

    @pl.when(i == 0)
    def _():
        issue(idx_ref, 0)

    @pl.when(i + 1 < pl.num_programs(0))
    def _():
        issue(idx_next_ref, 1 - slot)

    def wait(g, carry):
        for u in range(DMA_UNROLL):
            r = g * DMA_UNROLL + u
            _row_copy(src_hbm, 0, buf_ref.at[slot], r, sem.at[slot]).wait()
        return carry

    lax.fori_loop(0, rows // DMA_UNROLL, wait, 0)
    o_ref[...] = buf_ref[slot]


def _gather_rows(src, row_ids, rows_per_step):
    r = row_ids.shape[0]
    w = src.shape[1]
    nsteps = r // rows_per_step
    ids3 = row_ids.reshape(nsteps, 1, rows_per_step)
    return pl.pallas_call(
        functools.partial(_gather_body, rows=rows_per_step),
        grid=(nsteps,),
        in_specs=[pl.BlockSpec((1, 1, rows_per_step), lambda i: (i, 0, 0), memory_space=pltpu.SMEM),
                  pl.BlockSpec((1, 1, rows_per_step), lambda i: (jnp.minimum(i + 1, nsteps - 1), 0, 0),
                               memory_space=pltpu.SMEM),
                  pl.BlockSpec(memory_space=pl.ANY)],
        out_specs=pl.BlockSpec((rows_per_step, w), lambda i: (i, 0)),
        out_shape=jax.ShapeDtypeStruct((r, w), src.dtype),
        scratch_shapes=[pltpu.VMEM((2, rows_per_step, w), src.dtype), pltpu.SemaphoreType.DMA((2,))],
        compiler_params=_cparams(1),
        name="gather_rows",
    )(ids3, ids3, src)


MXU_W = 256


def _w1split_body(w_ref, p_ref, g_ref, l_ref):
    w = w_ref[...].astype(BF16)
    half = MXU_W // 2
    for c in range(w.shape[1] // MXU_W):
        r = jnp.dot(w[:, c * MXU_W:(c + 1) * MXU_W], p_ref[...], preferred_element_type=F32)
        g_ref[:, c * half:(c + 1) * half] = r[:, :half].astype(BF16)
        l_ref[:, c * half:(c + 1) * half] = r[:, half:].astype(BF16)


def _w1_split(w1):
    e, d, c2 = w1.shape
    tr, tcw = 512, 2048
    j = jnp.arange(MXU_W)
    perm = (j[None, :] == jnp.where(j % 2 == 0, j // 2, MXU_W // 2 + j // 2)[:, None]).astype(BF16)
    out = jax.ShapeDtypeStruct((e, d, c2 // 2), BF16)
    return pl.pallas_call(
        _w1split_body,
        grid=(e, d // tr, c2 // tcw),
        in_specs=[pl.BlockSpec((None, tr, tcw), lambda a, b, c: (a, b, c)),
                  pl.BlockSpec((MXU_W, MXU_W), lambda a, b, c: (0, 0))],
        out_specs=[pl.BlockSpec((None, tr, tcw // 2), lambda a, b, c: (a, b, c))] * 2,
        out_shape=[out, out],
        compiler_params=_cparams(3),
        name="w1_split",
    )(w1, perm)


def _moe_up_body(te_ref, nv_ref, x_ref, wg_ref, wl_ref, bg_ref, bl_ref, o_ref):
    i = pl.program_id(1)

    @pl.when(i < nv_ref[0])
    def _():
        w = x_ref[...]
        lo = _unpack2(w, 0).astype(BF16)
        hi = _unpack2(w, 1).astype(BF16)

        def up(w_ref, b_ref):
            return (jnp.dot(lo, w_ref[0:HALF, :], preferred_element_type=F32)
                    + jnp.dot(hi, w_ref[HALF:, :], preferred_element_type=F32) + b_ref[...])

        glu = jnp.minimum(up(wg_ref, bg_ref), SWIGLU_LIMIT)
        lin = jnp.clip(up(wl_ref, bl_ref), -SWIGLU_LIMIT, SWIGLU_LIMIT)
        o_ref[...] = (glu * _sigmoid(SWIGLU_ALPHA * glu) * (lin + 1.0)).astype(BF16)

    @pl.when(i >= nv_ref[0])
    def _():
        o_ref[...] = jnp.zeros_like(o_ref)


def _moe_up(tile_expert, n_valid, xs, w1g, w1l, b1g, b1l):
    r = xs.shape[0]
    tm, tn = MOE_TM, 1024
    nt = r // tm
    grid_spec = pltpu.PrefetchScalarGridSpec(
        num_scalar_prefetch=2,
        grid=(D_EXPERT // tn, nt),
        in_specs=[pl.BlockSpec((tm, HALF), lambda n, i, te, nv: (i, 0)),
                  pl.BlockSpec((None, D_MODEL, tn), lambda n, i, te, nv: (te[i], 0, n)),
                  pl.BlockSpec((None, D_MODEL, tn), lambda n, i, te, nv: (te[i], 0, n)),
                  pl.BlockSpec((None, 1, tn), lambda n, i, te, nv: (te[i], 0, n)),
                  pl.BlockSpec((None, 1, tn), lambda n, i, te, nv: (te[i], 0, n))],
        out_specs=pl.BlockSpec((tm, tn), lambda n, i, te, nv: (i, n)),
    )
    return pl.pallas_call(
        _moe_up_body, grid_spec=grid_spec,
        out_shape=jax.ShapeDtypeStruct((r, D_EXPERT), BF16),
        compiler_params=_cparams(2),
        name="moe_up",
    )(tile_expert, n_valid, xs, w1g, w1l, b1g, b1l)


def _moe_down_body(te_ref, nv_ref, a_ref, w_ref, b_ref, o_ref, wb_ref):
    i = pl.program_id(0)
    new_expert = jnp.logical_or(i == 0, te_ref[i] != te_ref[jnp.maximum(i - 1, 0)])

    @pl.when(new_expert)
    def _():
        wb_ref[...] = w_ref[...].astype(BF16)

    @pl.when(i < nv_ref[0])
    def _():
        y = jnp.dot(a_ref[...], wb_ref[...], preferred_element_type=F32) + b_ref[...]
        o_ref[...] = _pack2(y[:, :HALF], y[:, HALF:])

    @pl.when(i >= nv_ref[0])
    def _():
        zero = jnp.zeros(o_ref.shape, F32)
        o_ref[...] = _pack2(zero, zero)


def _moe_down(tile_expert, n_valid, act, w2b, b2):
    r = act.shape[0]
    tm = MOE_TM
    nt = r // tm
    grid_spec = pltpu.PrefetchScalarGridSpec(
        num_scalar_prefetch=2,
        grid=(nt,),
        in_specs=[pl.BlockSpec((tm, D_EXPERT), lambda i, te, nv: (i, 0)),
                  pl.BlockSpec((None, D_EXPERT, D_MODEL), lambda i, te, nv: (te[i], 0, 0)),
                  pl.BlockSpec((None, 1, D_MODEL), lambda i, te, nv: (te[i], 0, 0))],
        out_specs=pl.BlockSpec((tm, HALF), lambda i, te, nv: (i, 0)),
        scratch_shapes=[pltpu.VMEM((D_EXPERT, D_MODEL), BF16)],
    )
    return pl.pallas_call(
        _moe_down_body, grid_spec=grid_spec,
        out_shape=jax.ShapeDtypeStruct((r, HALF), U32),
        compiler_params=_cparams(1),
        name="moe_down",
    )(tile_expert, n_valid, act, w2b, b2)


def _combine_body(idx_ref, idx_next_ref, y_hbm, h1_ref, g_ref, o_ref, buf_ref, sem, *, tc):
    n = TOP_K * tc
    i = pl.program_id(0)
    slot = i % 2

    def issue(ids_ref, s):
        def start(g, carry):
            for u in range(DMA_UNROLL):
                a = g * DMA_UNROLL + u
                _row_copy(y_hbm, ids_ref[0, 0, a], buf_ref.at[s], a, sem.at[s]).start(priority=u % 2)
            return carry
        lax.fori_loop(0, n // DMA_UNROLL, start, 0)

    @pl.when(i == 0)
    def _():
        issue(idx_ref, 0)

    @pl.when(i + 1 < pl.num_programs(0))
    def _():
        issue(idx_next_ref, 1 - slot)

    def wait(g, carry):
        for u in range(DMA_UNROLL):
            a = g * DMA_UNROLL + u
            _row_copy(y_hbm, 0, buf_ref.at[slot], a, sem.at[slot]).wait()
        return carry

    lax.fori_loop(0, n // DMA_UNROLL, wait, 0)

    g = g_ref[...]
    lo = h1_ref[:, :HALF]
    hi = h1_ref[:, HALF:]
    for kk in range(TOP_K):
        w = buf_ref[slot, kk * tc:(kk + 1) * tc, :]
        gk = g[:, kk:kk + 1]
        lo = lo + gk * _unpack2(w, 0)
        hi = hi + gk * _unpack2(w, 1)
    o_ref[:, :HALF] = lo
    o_ref[:, HALF:] = hi


def _combine(pos_km, y_rows, h1, gates):
    t, d = h1.shape
    tc = 128
    nsteps = t // tc
    return pl.pallas_call(
        functools.partial(_combine_body, tc=tc),
        grid=(nsteps,),
        in_specs=[pl.BlockSpec((1, 1, TOP_K * tc), lambda i: (i, 0, 0), memory_space=pltpu.SMEM),
                  pl.BlockSpec((1, 1, TOP_K * tc), lambda i: (jnp.minimum(i + 1, nsteps - 1), 0, 0),
                               memory_space=pltpu.SMEM),
                  pl.BlockSpec(memory_space=pl.ANY),
                  pl.BlockSpec((tc, d), lambda i: (i, 0)),
                  pl.BlockSpec((tc, LANES), lambda i: (i, 0))],
        out_specs=pl.BlockSpec((tc, d), lambda i: (i, 0)),
        out_shape=jax.ShapeDtypeStruct((t, d), F32),
        scratch_shapes=[pltpu.VMEM((2, TOP_K * tc, HALF), U32), pltpu.SemaphoreType.DMA((2,))],
        compiler_params=_cparams(1),
        name="moe_combine",
    )(pos_km, pos_km, y_rows, h1, gates)


def _route(ti, counts, tm):
    t = ti.shape[0]
    n_assign = t * TOP_K
    n_rows = n_assign + N_EXPERTS * tm
    nt = n_rows // tm
    top_e = ti[:, :TOP_K]
    rank = ti[:, TOP_K:2 * TOP_K]
    counts = counts[0, :N_EXPERTS]
    tiles_per = (counts + (tm - 1)) // tm
    tile_end = jnp.cumsum(tiles_per)
    pstart = (tile_end - tiles_per) * tm
    experts = jnp.arange(N_EXPERTS, dtype=I32)
    onehot = top_e[:, :, None] == experts[None, None, :]
    pos = jnp.sum(jnp.where(onehot, pstart[None, None, :], 0), axis=-1) + rank
    token_of = jnp.arange(n_assign, dtype=I32) // TOP_K
    row_token = jnp.zeros((n_rows,), I32).at[pos.reshape(-1)].set(token_of, mode="drop", unique_indices=True)
    n_valid = tile_end[N_EXPERTS - 1]
    tile_ids = jnp.minimum(jnp.arange(nt, dtype=I32), n_valid - 1)
    tile_expert = jnp.sum((tile_end[None, :] <= tile_ids[:, None]).astype(I32), axis=1)
    tile_expert = jnp.minimum(tile_expert, N_EXPERTS - 1)
    return row_token, pos, tile_expert, n_valid.reshape(1)


def _layer(h, cos_t, sin_t, lam_init, norm1_g, w_in, conv_w, conv_b, dt_bias, a_log, d_skip, ssm_norm_g,
           q_norm_g, k_norm_g, lq1, lk1, lq2, lk2, subln_g, w_out, norm2_g, w_router, b_router, w1, b1, w2, b2):
    bsz, seq, d = h.shape
    t = bsz * seq
    x2 = h.reshape(t, d)

    o_z = 0
    o_xbc = D_SSM
    o_dt = o_xbc + D_SSM + 2 * BC_W
    o_q = o_dt + SSM_HEADS
    o_k = o_q + D_QK
    o_v = o_k + D_QK
    w_p = jnp.concatenate([
        w_in[:, o_z:o_z + D_SSM], w_in[:, o_xbc:o_xbc + D_SSM],
        w_in[:, o_q:o_q + D_QK], w_in[:, o_k:o_k + D_QK], w_in[:, o_v:o_v + D_V],
        w_in[:, o_xbc + D_SSM:o_xbc + D_SSM + 2 * BC_W],
        jnp.pad(w_in[:, o_dt:o_dt + SSM_HEADS], ((0, 0), (0, LANES - SSM_HEADS + PROJ_PAD)))],
        axis=1).astype(BF16)

    proj = _in_proj(x2, norm1_g[None, :], w_p)
    y_ssm = _ssd(proj, bsz, seq, conv_w, conv_b, dt_bias, a_log, d_skip, ssm_norm_g)

    reps = D_QK // DIFF_HEAD_DIM
    qr, kr, vb = _qk_prep(proj, cos_t, sin_t, jnp.tile(q_norm_g, reps)[None, :], jnp.tile(k_norm_g, reps)[None, :])
    lam = (jnp.exp(jnp.sum(lq1 * lk1)) - jnp.exp(jnp.sum(lq2 * lk2)) + lam_init).reshape(1).astype(F32)
    y_attn = _attention(qr, kr, vb, lam, subln_g, bsz, seq, 1.0 - lam_init)

    w_ob = w_out.astype(BF16)
    w_r = jnp.pad(w_router, ((0, 0), (0, LANES - N_EXPERTS)))
    b_r = jnp.pad(b_router, (0, LANES - N_EXPERTS))[None, :]
    h1, h2p, ti, tg, counts = _out_proj_router(x2, y_ssm, y_attn, w_ob[:D_SSM], w_ob[D_SSM:], norm2_g[None, :],
                                               w_r, b_r)

    row_token, pos, tile_expert, n_valid = _route(ti, counts, MOE_TM)
    xs = _gather_rows(h2p, row_token, 2 * MOE_TM)
    w1g, w1l = _w1_split(w1)
    act = _moe_up(tile_expert, n_valid, xs, w1g, w1l, b1[:, None, 0::2], b1[:, None, 1::2])
    y_rows = _moe_down(tile_expert, n_valid, act, w2, b2[:, None, :])

    tc = 128
    pos_km = pos.reshape(t // tc, tc, TOP_K).transpose(0, 2, 1).reshape(t // tc, 1, TOP_K * tc)
    out = _combine(pos_km, y_rows, h1, tg)
    return out.reshape(bsz, seq, d)


def kernel(x, positions, norm1_g, w_in, conv_w, conv_b, dt_bias, a_log, d_skip, ssm_norm_g, q_norm_g, k_norm_g,
           lambda_q1, lambda_k1, lambda_q2, lambda_k2, subln_g, w_out, norm2_g, w_router, b_router, w1, b1, w2, b2):
    bsz, seq, _ = x.shape
    inv = ROPE_THETA ** (-jnp.arange(0, DIFF_HEAD_DIM, 2, dtype=F32) / DIFF_HEAD_DIM)
    ang = positions.astype(F32).reshape(bsz * seq, 1) * inv[None, :]
    cos, sin = jnp.cos(ang), jnp.sin(ang)
    cos_t = jnp.tile(cos, (1, 2 * LANES // DIFF_HEAD_DIM))
    sin_t = jnp.tile(jnp.concatenate([-sin, sin], axis=1), (1, LANES // DIFF_HEAD_DIM))

    h = x
    for layer in range(norm1_g.shape[0]):
        lam_init = 0.8 - 0.6 * math.exp(-0.3 * layer)
        h = _layer(h, cos_t, sin_t, lam_init, norm1_g[layer], w_in[layer], conv_w[layer], conv_b[layer],
                   dt_bias[layer], a_log[layer], d_skip[layer], ssm_norm_g[layer], q_norm_g[layer],
                   k_norm_g[layer], lambda_q1[layer], lambda_k1[layer], lambda_q2[layer], lambda_k2[layer],
                   subln_g[layer], w_out[layer], norm2_g[layer], w_router[layer], b_router[layer],
                   w1[layer], b1[layer], w2[layer], b2[layer])
    return h
```

```python
import functools
import math

import jax
import jax.numpy as jnp
from jax import lax
from jax.experimental import pallas as pl
from jax.experimental.pallas import tpu as pltpu

F32 = jnp.float32
BF16 = jnp.bfloat16
U32 = jnp.uint32
I32 = jnp.int32
HIGHEST = lax.Precision.HIGHEST

D_MODEL = 2048
D_SSM = 1024
D_ATTN = 1024
SSM_HEAD_DIM = 64
SSM_HEADS = 16
SSM_GROUPS = 2
SSM_STATE = 128
CONV_WIDTH = 4
CHUNK = 128
DIFF_HEAD_DIM = 64
DIFF_V_DIM = 128
DIFF_HEADS = 8
D_QK = 1024
D_V = 1024
ROPE_THETA = 10000.0
N_EXPERTS = 32
TOP_K = 4
D_EXPERT = 2048
SWIGLU_ALPHA = 1.702
SWIGLU_LIMIT = 7.0
NORM_EPS = 1e-5
LANES = 128
SUBLANES = 8
HALF = D_MODEL // 2

BC_W = SSM_GROUPS * SSM_STATE
PROJ_COLS = D_SSM + D_SSM + D_QK + D_QK + D_V + 2 * BC_W + LANES
INPROJ_TN = 1536
PROJ_PAD = -PROJ_COLS % INPROJ_TN

MOE_TM = 256
VMEM_LIMIT = 56 * 1024 * 1024


def _cparams(n_axes, vmem=VMEM_LIMIT):
    return pltpu.CompilerParams(dimension_semantics=("arbitrary",) * n_axes, vmem_limit_bytes=vmem)


def _sigmoid(v):
    return 1.0 / (1.0 + jnp.exp(-v))


def _pack2(lo, hi):
    return pltpu.pack_elementwise([lo, hi], packed_dtype=BF16)


def _unpack2(w, index):
    return pltpu.unpack_elementwise(w, index=index, packed_dtype=BF16, unpacked_dtype=F32)


def _inproj_body(x_ref, g_ref, w_ref, o_ref, u_ref):
    @pl.when(pl.program_id(1) == 0)
    def _():
        x = x_ref[...]
        ms = jnp.mean(x * x, axis=-1, keepdims=True)
        u_ref[...] = (x * lax.rsqrt(ms + NORM_EPS) * g_ref[...]).astype(BF16)

    o_ref[...] = jnp.dot(u_ref[...], w_ref[...], preferred_element_type=F32)


def _in_proj(x2, g, w_p):
    t, d = x2.shape
    ncol = w_p.shape[1]
    tm, tn = 512, INPROJ_TN
    return pl.pallas_call(
        _inproj_body,
        grid=(t // tm, ncol // tn),
        in_specs=[pl.BlockSpec((tm, d), lambda i, j: (i, 0)),
                  pl.BlockSpec((1, d), lambda i, j: (0, 0)),
                  pl.BlockSpec((d, tn), lambda i, j: (0, j))],
        out_specs=pl.BlockSpec((tm, tn), lambda i, j: (i, j)),
        out_shape=jax.ShapeDtypeStruct((t, ncol), F32),
        scratch_shapes=[pltpu.VMEM((tm, d), BF16)],
        compiler_params=_cparams(2),
        name="in_proj",
    )(x2, g, w_p)


def _ssd_body(z_ref, x_ref, b_ref, c_ref, dt_ref, cwx_ref, cwb_ref, cwc_ref, cbx_ref, cbb_ref, cbc_ref,
              dtb_ref, alog_ref, dsk_ref, ng_ref, e_ref, y_ref, tx_ref, tb_ref, tc_ref, st_ref):
    @pl.when(pl.program_id(1) == 0)
    def _():
        tx_ref[...] = jnp.zeros_like(tx_ref)
        tb_ref[...] = jnp.zeros_like(tb_ref)
        tc_ref[...] = jnp.zeros_like(tc_ref)
        st_ref[...] = jnp.zeros_like(st_ref)

    row8 = lax.broadcasted_iota(I32, (SUBLANES, 1), 0)

    def conv_silu(u_ref, t_ref, w_ref, bias_ref):
        u = u_ref[...]
        tail = t_ref[...]
        w = w_ref[...]
        acc = u * w[CONV_WIDTH - 1:CONV_WIDTH, :] + bias_ref[...]
        for s in range(1, CONV_WIDTH):
            ru = pltpu.roll(u, s, 0)
            rt = pltpu.roll(tail, s, 0)
            head = jnp.where(row8 < s, rt, ru[0:SUBLANES, :])
            sh = jnp.concatenate([head, ru[SUBLANES:, :]], axis=0)
            acc = acc + sh * w[CONV_WIDTH - 1 - s:CONV_WIDTH - s, :]
        t_ref[...] = u[CHUNK - SUBLANES:CHUNK, :]
        return acc * _sigmoid(acc)

    xs = conv_silu(x_ref, tx_ref, cwx_ref, cbx_ref)
    bm = conv_silu(b_ref, tb_ref, cwb_ref, cbb_ref)
    cm = conv_silu(c_ref, tc_ref, cwc_ref, cbc_ref)

    dtr = dt_ref[...] + dtb_ref[...]
    dt = jnp.maximum(dtr, 0.0) + jnp.log1p(jnp.exp(-jnp.abs(dtr)))
    da = dt * (-jnp.exp(alog_ref[...]))

    rowi = lax.broadcasted_iota(I32, (CHUNK, 1), 0)
    acs = da
    k = 1
    while k < CHUNK:
        acs = acs + jnp.where(rowi >= k, pltpu.roll(acs, k, 0), 0.0)
        k *= 2
    acs_t = acs.T

    ex = jnp.dot(jnp.concatenate([dt, acs], axis=0), e_ref[...], precision=HIGHEST,
                 preferred_element_type=F32)
    dt_f = ex[:CHUNK]
    a_f = ex[CHUNK:]
    a_last = a_f[CHUNK - 1:CHUNK, :]
    expa = jnp.exp(a_f)
    dstate = jnp.exp(a_last - a_f)
    cdec = jnp.exp(a_last)

    xdt = xs * dt_f
    xdt_b = xdt.astype(BF16)
    xds_b = (xdt * dstate).astype(BF16)

    coli = lax.broadcasted_iota(I32, (1, CHUNK), 1)
    tril = rowi >= coli
    lane = lax.broadcasted_iota(I32, (1, LANES), 1)
    hpg = SSM_HEADS // SSM_GROUPS
    gw = hpg * SSM_HEAD_DIM
    y_groups = []
    for g in range(SSM_GROUPS):
        bm_g = bm[:, g * SSM_STATE:(g + 1) * SSM_STATE]
        cm_g = cm[:, g * SSM_STATE:(g + 1) * SSM_STATE]
        cm_b = cm_g.astype(BF16)
        cb = lax.dot_general(cm_b, bm_g.astype(BF16), (((1,), (1,)), ((), ())),
                             preferred_element_type=F32)
        parts = []
        for pp in range(hpg // 2):
            p = g * (hpg // 2) + pp
            xp = xdt_b[:, p * LANES:(p + 1) * LANES]
            ys = []
            for hh in range(2):
                h = 2 * p + hh
                seg = acs[:, h:h + 1] - acs_t[h:h + 1, :]
                lm = jnp.where(tril, jnp.exp(seg), 0.0)
                ys.append(jnp.dot((cb * lm).astype(BF16), xp, preferred_element_type=F32))
            parts.append(jnp.where(lane < SSM_HEAD_DIM, ys[0], ys[1]))
        y_diag = jnp.concatenate(parts, axis=1)
        st_g = st_ref[:, g * gw:(g + 1) * gw]
        y_off = jnp.dot(cm_b, st_g.astype(BF16), preferred_element_type=F32) * expa[:, g * gw:(g + 1) * gw]
        new = jnp.dot(bm_g.T.astype(BF16), xds_b[:, g * gw:(g + 1) * gw], preferred_element_type=F32)
        st_ref[:, g * gw:(g + 1) * gw] = st_g * cdec[:, g * gw:(g + 1) * gw] + new
        y_groups.append(y_diag + y_off)

    zz = z_ref[...]
    gate = zz * _sigmoid(zz)
    dsk = dsk_ref[...]
    ng = ng_ref[...]
    outs = []
    for g in range(SSM_GROUPS):
        sl = slice(g * gw, (g + 1) * gw)
        yg = (y_groups[g] + xs[:, sl] * dsk[:, sl]) * gate[:, sl]
        ms = jnp.mean(yg * yg, axis=-1, keepdims=True)
        outs.append(yg * lax.rsqrt(ms + NORM_EPS) * ng[:, sl])
    y_ref[...] = jnp.concatenate(outs, axis=1).astype(BF16)


def _ssd(proj, bsz, seq, cw, cb, dt_bias, a_log, d_skip, norm_g):
    t = bsz * seq
    nc = seq // CHUNK
    cwx, cwb, cwc = cw[:, :D_SSM], cw[:, D_SSM:D_SSM + BC_W], cw[:, D_SSM + BC_W:]
    cb = cb[None, :]
    cbx, cbb, cbc = cb[:, :D_SSM], cb[:, D_SSM:D_SSM + BC_W], cb[:, D_SSM + BC_W:]
    pad = LANES - SSM_HEADS
    dtb = jnp.pad(dt_bias, (0, pad))[None, :]
    alog = jnp.pad(a_log, (0, pad))[None, :]
    dsk = jnp.repeat(d_skip, SSM_HEAD_DIM)[None, :]
    expand = (jnp.arange(D_SSM)[None, :] // SSM_HEAD_DIM == jnp.arange(LANES)[:, None]).astype(F32)

    def row(b, c):
        return b * nc + c

    def full(shape):
        return pl.BlockSpec(shape, lambda b, c: (0, 0))

    bc0 = (2 * D_SSM + 2 * D_QK + D_V) // BC_W
    dt0 = (PROJ_COLS - LANES) // LANES
    return pl.pallas_call(
        _ssd_body,
        grid=(bsz, nc),
        in_specs=[pl.BlockSpec((CHUNK, D_SSM), lambda b, c: (row(b, c), 0)),
                  pl.BlockSpec((CHUNK, D_SSM), lambda b, c: (row(b, c), 1)),
                  pl.BlockSpec((CHUNK, BC_W), lambda b, c: (row(b, c), bc0)),
                  pl.BlockSpec((CHUNK, BC_W), lambda b, c: (row(b, c), bc0 + 1)),
                  pl.BlockSpec((CHUNK, LANES), lambda b, c: (row(b, c), dt0)),
                  full((CONV_WIDTH, D_SSM)), full((CONV_WIDTH, BC_W)), full((CONV_WIDTH, BC_W)),
                  full((1, D_SSM)), full((1, BC_W)), full((1, BC_W)),
                  full((1, LANES)), full((1, LANES)), full((1, D_SSM)), full((1, D_SSM)),
                  full((LANES, D_SSM))],
        out_specs=pl.BlockSpec((CHUNK, D_SSM), lambda b, c: (row(b, c), 0)),
        out_shape=jax.ShapeDtypeStruct((t, D_SSM), BF16),
        scratch_shapes=[pltpu.VMEM((SUBLANES, D_SSM), F32), pltpu.VMEM((SUBLANES, BC_W), F32),
                        pltpu.VMEM((SUBLANES, BC_W), F32), pltpu.VMEM((SSM_STATE, D_SSM), F32)],
        compiler_params=_cparams(2),
        name="ssd",
    )(proj, proj, proj, proj, proj, cwx, cwb, cwc, cbx, cbb, cbc, dtb, alog, dsk, norm_g[None, :], expand)


def _qkprep_body(q_ref, k_ref, v_ref, cos_ref, sin_ref, gq_ref, gk_ref, gm_ref, qo_ref, ko_ref, vo_ref):
    reps = D_QK // LANES
    cos = jnp.tile(cos_ref[...], (1, reps))
    sin = jnp.tile(sin_ref[...], (1, reps))
    lane = lax.broadcasted_iota(I32, (1, D_QK), 1)
    first = (lane & (DIFF_HEAD_DIM - 1)) < DIFF_HEAD_DIM // 2
    gm = gm_ref[...]

    def prep(t, g, scale):
        sq = t * t
        hi = sq.astype(BF16)
        lo = (sq - hi.astype(F32)).astype(BF16)
        ss = jnp.dot(hi, gm, preferred_element_type=F32) + jnp.dot(lo, gm, preferred_element_type=F32)
        tn = t * lax.rsqrt(ss * (1.0 / DIFF_HEAD_DIM) + NORM_EPS) * g
        half = DIFF_HEAD_DIM // 2
        rot = jnp.where(first, pltpu.roll(tn, D_QK - half, 1), pltpu.roll(tn, half, 1))
        return (tn * cos + rot * sin) * scale

    qo_ref[...] = prep(q_ref[...], gq_ref[...], DIFF_HEAD_DIM ** -0.5 * math.log2(math.e)).astype(BF16)
    ko_ref[...] = prep(k_ref[...], gk_ref[...], 1.0).astype(BF16)
    vo_ref[...] = v_ref[...].astype(BF16)


def _qk_prep(proj, cos_t, sin_t, gq, gk):
    t = proj.shape[0]
    tm = 256
    gsum = (jnp.arange(D_QK)[:, None] // DIFF_HEAD_DIM == jnp.arange(D_QK)[None, :] // DIFF_HEAD_DIM).astype(BF16)
    out = jax.ShapeDtypeStruct((t, D_QK), BF16)
    return pl.pallas_call(
        _qkprep_body,
        grid=(t // tm,),
        in_specs=[pl.BlockSpec((tm, D_QK), lambda i: (i, 2)),
                  pl.BlockSpec((tm, D_QK), lambda i: (i, 3)),
                  pl.BlockSpec((tm, D_V), lambda i: (i, 4)),
                  pl.BlockSpec((tm, LANES), lambda i: (i, 0)),
                  pl.BlockSpec((tm, LANES), lambda i: (i, 0)),
                  pl.BlockSpec((1, D_QK), lambda i: (0, 0)),
                  pl.BlockSpec((1, D_QK), lambda i: (0, 0)),
                  pl.BlockSpec((D_QK, D_QK), lambda i: (0, 0))],
        out_specs=[pl.BlockSpec((tm, D_QK), lambda i: (i, 0))] * 3,
        out_shape=[out, out, out],
        compiler_params=_cparams(1),
        name="qk_prep",
    )(proj, proj, proj, cos_t, sin_t, gq, gk, gsum)


KV_UNROLL = 4


def _attn_body(lam_ref, q_ref, k_ref, v_ref, sg_ref, o_ref, qs_ref, vx_ref, m_ref, acc_ref, *, tq, row_chunk,
               out_scale):
    i = pl.program_id(2)

    @pl.when(i == 0)
    def _():
        vx_ref[:, 0:LANES] = v_ref[...]
        vx_ref[:, LANES:2 * LANES] = jnp.ones((vx_ref.shape[0], LANES), BF16)

    q = q_ref[...]
    lane = lax.broadcasted_iota(I32, (1, LANES), 1)
    zero = jnp.zeros_like(q)
    qs_ref[0:tq, :] = jnp.where(lane < DIFF_HEAD_DIM, q, zero)
    qs_ref[tq:2 * tq, :] = jnp.where(lane >= DIFF_HEAD_DIM, q, zero)
    m_ref[...] = jnp.full_like(m_ref, -jnp.inf)
    acc_ref[...] = jnp.zeros_like(acc_ref)

    def step(j, masked):
        start = pl.multiple_of(j * tq, tq)
        k = k_ref[pl.ds(start, tq), :]
        vx = vx_ref[pl.ds(start, tq), :]
        for rc in range(2 * tq // row_chunk):
            rows = slice(rc * row_chunk, (rc + 1) * row_chunk)
            s = lax.dot_general(qs_ref[rows, :], k, (((1,), (1,)), ((), ())), preferred_element_type=F32)
            if masked:
                r = lax.broadcasted_iota(I32, (row_chunk, tq), 0) + (rc * row_chunk) % tq
                c = lax.broadcasted_iota(I32, (row_chunk, tq), 1)
                s = jnp.where(c <= r, s, -jnp.inf)
            m_prev = m_ref[rows, :]
            m_new = jnp.maximum(m_prev, jnp.max(s, axis=1, keepdims=True))
            alpha = jnp.exp2(m_prev - m_new)
            p = jnp.exp2(s - jnp.tile(m_new, (1, tq // LANES)))
            pv = jnp.dot(p.astype(BF16), vx, preferred_element_type=F32)
            acc_ref[rows, :] = jnp.tile(alpha, (1, 2)) * acc_ref[rows, :] + pv
            m_ref[rows, :] = m_new

    def off_diagonal_group(jj, carry):
        for u in range(KV_UNROLL):
            step(KV_UNROLL * jj + u, False)
        return carry

    lax.fori_loop(0, i // KV_UNROLL, off_diagonal_group, 0)
    done = (i // KV_UNROLL) * KV_UNROLL
    for rem in range(KV_UNROLL):
        @pl.when(i - done == rem)
        def _(rem=rem):
            for u in range(rem):
                step(done + u, False)
            step(i, True)

    acc = acc_ref[...]
    o = acc[:, 0:LANES] / acc[:, LANES:2 * LANES]
    d = o[0:tq, :] - lam_ref[0] * o[tq:2 * tq, :]
    ms = jnp.mean(d * d, axis=-1, keepdims=True)
    o_ref[...] = (d * lax.rsqrt(ms + NORM_EPS) * sg_ref[...] * out_scale).astype(BF16)


def _attention(qr, kr, vb, lam, subln_g, bsz, seq, out_scale):
    t = bsz * seq
    tq = 512
    nq = seq // tq
    body = functools.partial(_attn_body, tq=tq, row_chunk=128, out_scale=out_scale)
    grid_spec = pltpu.PrefetchScalarGridSpec(
        num_scalar_prefetch=1,
        grid=(bsz, DIFF_HEADS, nq),
        in_specs=[pl.BlockSpec((tq, LANES), lambda b, h, i, lam: (b * nq + i, h)),
                  pl.BlockSpec((seq, LANES), lambda b, h, i, lam: (b, h)),
                  pl.BlockSpec((seq, LANES), lambda b, h, i, lam: (b, h)),
                  pl.BlockSpec((1, LANES), lambda b, h, i, lam: (0, 0))],
        out_specs=pl.BlockSpec((tq, LANES), lambda b, h, i, lam: (b * nq + i, h)),
        scratch_shapes=[pltpu.VMEM((2 * tq, LANES), BF16), pltpu.VMEM((seq, 2 * LANES), BF16),
                        pltpu.VMEM((2 * tq, LANES), F32), pltpu.VMEM((2 * tq, 2 * LANES), F32)],
    )
    return pl.pallas_call(
        body, grid_spec=grid_spec,
        out_shape=jax.ShapeDtypeStruct((t, D_ATTN), BF16),
        compiler_params=_cparams(3),
        name="diff_attention",
    )(lam, qr, kr, vb, subln_g[None, :])


def _outproj_body(x_ref, ys_ref, ya_ref, wt_ref, wb_ref, g2_ref, wrh_ref, wrl_ref, br_ref, ltri_ref,
                  h1_ref, hp_ref, ti_ref, tg_ref, cnt_out_ref, cnt_ref):
    h1 = (x_ref[...] + jnp.dot(ys_ref[...], wt_ref[...], preferred_element_type=F32)
          + jnp.dot(ya_ref[...], wb_ref[...], preferred_element_type=F32))
    h1_ref[...] = h1
    ms = jnp.mean(h1 * h1, axis=-1, keepdims=True)
    h2 = h1 * lax.rsqrt(ms + NORM_EPS) * g2_ref[...]
    hp_ref[...] = _pack2(h2[:, :HALF], h2[:, HALF:])

    h2_hi = h2.astype(BF16)
    h2_lo = (h2 - h2_hi.astype(F32)).astype(BF16)
    logits = (jnp.dot(h2_hi, wrh_ref[...], preferred_element_type=F32)
              + jnp.dot(h2_hi, wrl_ref[...], preferred_element_type=F32)
              + jnp.dot(h2_lo, wrh_ref[...], preferred_element_type=F32) + br_ref[...])
    lane = lax.broadcasted_iota(I32, (1, LANES), 1)
    lanef = lane.astype(F32)
    cur = jnp.where(lane < N_EXPERTS, logits, -jnp.inf)
    vals, idxs = [], []
    for _ in range(TOP_K):
        m = jnp.max(cur, axis=1, keepdims=True)
        idx = jnp.min(jnp.where(cur == m, lanef, float(LANES)), axis=1, keepdims=True)
        vals.append(m)
        idxs.append(idx)
        cur = jnp.where(lanef == idx, -jnp.inf, cur)
    es = [jnp.exp(v - vals[0]) for v in vals]
    den = es[0] + es[1] + es[2] + es[3]

    @pl.when(pl.program_id(0) == 0)
    def _():
        cnt_ref[...] = jnp.zeros_like(cnt_ref)

    base = cnt_ref[...]
    ltri = ltri_ref[...]
    ti = jnp.zeros(logits.shape, F32)
    tg = jnp.zeros(logits.shape, F32)
    for kk in range(TOP_K):
        oh = lanef == idxs[kk]
        ohf = jnp.where(oh, 1.0, 0.0)
        pre = jnp.dot(ltri, ohf.astype(BF16), preferred_element_type=F32)
        rank = jnp.sum(jnp.where(oh, pre + base, 0.0), axis=1, keepdims=True)
        base = base + jnp.sum(ohf, axis=0, keepdims=True)
        ti = jnp.where(lane == kk, idxs[kk], ti)
        ti = jnp.where(lane == TOP_K + kk, rank, ti)
        tg = jnp.where(lane == kk, es[kk] / den, tg)
    cnt_ref[...] = base
    ti_ref[...] = ti.astype(I32)
    tg_ref[...] = tg
    cnt_out_ref[...] = jnp.broadcast_to(base, cnt_out_ref.shape).astype(I32)


def _out_proj_router(x2, y_ssm, y_attn, w_top, w_bot, g2, w_r, b_r):
    t, d = x2.shape
    tm = 256
    w_rh = w_r.astype(BF16)
    w_rl = (w_r - w_rh.astype(F32)).astype(BF16)
    ltri = (jnp.arange(tm)[:, None] > jnp.arange(tm)[None, :]).astype(BF16)

    def full(shape):
        return pl.BlockSpec(shape, lambda i: (0, 0))

    return pl.pallas_call(
        _outproj_body,
        grid=(t // tm,),
        in_specs=[pl.BlockSpec((tm, d), lambda i: (i, 0)),
                  pl.BlockSpec((tm, D_SSM), lambda i: (i, 0)),
                  pl.BlockSpec((tm, D_ATTN), lambda i: (i, 0)),
                  full((D_SSM, d)), full((D_ATTN, d)), full((1, d)), full((d, LANES)), full((d, LANES)),
                  full((1, LANES)), full((tm, tm))],
        out_specs=[pl.BlockSpec((tm, d), lambda i: (i, 0)),
                   pl.BlockSpec((tm, HALF), lambda i: (i, 0)),
                   pl.BlockSpec((tm, LANES), lambda i: (i, 0)),
                   pl.BlockSpec((tm, LANES), lambda i: (i, 0)),
                   full((SUBLANES, LANES))],
        out_shape=[jax.ShapeDtypeStruct((t, d), F32), jax.ShapeDtypeStruct((t, HALF), U32),
                   jax.ShapeDtypeStruct((t, LANES), I32), jax.ShapeDtypeStruct((t, LANES), F32),
                   jax.ShapeDtypeStruct((SUBLANES, LANES), I32)],
        scratch_shapes=[pltpu.VMEM((1, LANES), F32)],
        compiler_params=_cparams(1),
        name="out_proj_router",
    )(x2, y_ssm, y_attn, w_top, w_bot, g2, w_rh, w_rl, b_r, ltri)


def _row_copy(src_hbm, src_row, dst_ref, dst_row, sem):
    return pltpu.make_async_copy(src_hbm.at[pl.ds(src_row, 1), :], dst_ref.at[pl.ds(dst_row, 1), :], sem)


DMA_UNROLL = 8


def _zero_rows_body(o_ref):
    zero = jnp.zeros(o_ref.shape, F32)
    o_ref[...] = _pack2(zero, zero)


def _zero_rows(n_rows, rows_per_step):
    return pl.pallas_call(
        _zero_rows_body,
        grid=(n_rows // rows_per_step,),
        out_specs=pl.BlockSpec((rows_per_step, HALF), lambda i: (i, 0)),
        out_shape=jax.ShapeDtypeStruct((n_rows, HALF), U32),
        compiler_params=_cparams(1),
        name="zero_rows",
    )()


def _dispatch_body(pos_ref, x_ref, init_hbm, dst_hbm, sem, *, tokens):
    del init_hbm
    n = TOP_K * tokens

    def start(g, carry):
        for u in range(DMA_UNROLL):
            a = g * DMA_UNROLL + u
            t = g * (DMA_UNROLL // TOP_K) + u // TOP_K
            _row_copy(x_ref, t, dst_hbm, pos_ref[0, 0, a], sem).start(priority=u % 2)
        return carry

    def wait(g, carry):
        for u in range(DMA_UNROLL):
            t = g * (DMA_UNROLL // TOP_K) + u // TOP_K
            _row_copy(x_ref, t, dst_hbm, 0, sem).wait()
        return carry

    lax.fori_loop(0, n // DMA_UNROLL, start, 0)
    lax.fori_loop(0, n // DMA_UNROLL, wait, 0)


def _dispatch_rows(src, pos, n_rows):
    t, w = src.shape
    tokens = 256
    nsteps = t // tokens
    return pl.pallas_call(
        functools.partial(_dispatch_body, tokens=tokens),
        grid=(nsteps,),
        in_specs=[pl.BlockSpec((1, 1, TOP_K * tokens), lambda i: (i, 0, 0), memory_space=pltpu.SMEM),
                  pl.BlockSpec((tokens, w), lambda i: (i, 0)),
                  pl.BlockSpec(memory_space=pl.ANY)],
        out_specs=pl.BlockSpec(memory_space=pl.ANY),
        out_shape=jax.ShapeDtypeStruct((n_rows, w), src.dtype),
        scratch_shapes=[pltpu.SemaphoreType.DMA(())],
        input_output_aliases={2: 0},
        compiler_params=_cparams(1),
        name="dispatch_rows",
    )(pos.reshape(nsteps, 1, TOP_K * tokens), src, _zero_rows(n_rows, 2048))


MXU_W = 256


def _moe_up_body(te_ref, nv_ref, x_ref, w_ref, p_ref, bg_ref, bl_ref, o_ref, wg_ref, wl_ref):
    i = pl.program_id(1)
    new_expert = jnp.logical_or(i == 0, te_ref[i] != te_ref[jnp.maximum(i - 1, 0)])

    @pl.when(new_expert)
    def _():
        half = MXU_W // 2
        for c in range(w_ref.shape[1] // MXU_W):
            wc = w_ref[:, c * MXU_W:(c + 1) * MXU_W].astype(BF16)
            r = jnp.dot(wc, p_ref[...], preferred_element_type=F32)
            wg_ref[:, c * half:(c + 1) * half] = r[:, :half].astype(BF16)
            wl_ref[:, c * half:(c + 1) * half] = r[:, half:].astype(BF16)

    @pl.when(i < nv_ref[0])
    def _():
        w = x_ref[...]
        lo = _unpack2(w, 0).astype(BF16)
        hi = _unpack2(w, 1).astype(BF16)

        def up(ws_ref, b_ref):
            return (jnp.dot(lo, ws_ref[0:HALF, :], preferred_element_type=F32)
                    + jnp.dot(hi, ws_ref[HALF:, :], preferred_element_type=F32) + b_ref[...])

        glu = jnp.minimum(up(wg_ref, bg_ref), SWIGLU_LIMIT)
        lin = jnp.clip(up(wl_ref, bl_ref), -SWIGLU_LIMIT, SWIGLU_LIMIT)
        o_ref[...] = (glu * _sigmoid(SWIGLU_ALPHA * glu) * (lin + 1.0)).astype(BF16)

    @pl.when(i >= nv_ref[0])
    def _():
        o_ref[...] = jnp.zeros_like(o_ref)


def _moe_up(tile_expert, n_valid, xs, w1, b1g, b1l):
    r = xs.shape[0]
    tm, tn = MOE_TM, 1024
    nt = r // tm
    j = jnp.arange(MXU_W)
    perm = (j[None, :] == jnp.where(j % 2 == 0, j // 2, MXU_W // 2 + j // 2)[:, None]).astype(BF16)
    grid_spec = pltpu.PrefetchScalarGridSpec(
        num_scalar_prefetch=2,
        grid=(D_EXPERT // tn, nt),
        in_specs=[pl.BlockSpec((tm, HALF), lambda n, i, te, nv: (i, 0)),
                  pl.BlockSpec((None, D_MODEL, 2 * tn), lambda n, i, te, nv: (te[i], 0, n)),
                  pl.BlockSpec((MXU_W, MXU_W), lambda n, i, te, nv: (0, 0)),
                  pl.BlockSpec((None, 1, tn), lambda n, i, te, nv: (te[i], 0, n)),
                  pl.BlockSpec((None, 1, tn), lambda n, i, te, nv: (te[i], 0, n))],
        out_specs=pl.BlockSpec((tm, tn), lambda n, i, te, nv: (i, n)),
        scratch_shapes=[pltpu.VMEM((D_MODEL, tn), BF16), pltpu.VMEM((D_MODEL, tn), BF16)],
    )
    return pl.pallas_call(
        _moe_up_body, grid_spec=grid_spec,
        out_shape=jax.ShapeDtypeStruct((r, D_EXPERT), BF16),
        compiler_params=_cparams(2),
        name="moe_up",
    )(tile_expert, n_valid, xs, w1, perm, b1g, b1l)


def _moe_down_body(te_ref, nv_ref, a_ref, w_ref, b_ref, o_ref, wb_ref):
    i = pl.program_id(0)
    new_expert = jnp.logical_or(i == 0, te_ref[i] != te_ref[jnp.maximum(i - 1, 0)])

    @pl.when(new_expert)
    def _():
        wb_ref[...] = w_ref[...].astype(BF16)

    @pl.when(i < nv_ref[0])
    def _():
        y = jnp.dot(a_ref[...], wb_ref[...], preferred_element_type=F32) + b_ref[...]
        o_ref[...] = _pack2(y[:, :HALF], y[:, HALF:])

    @pl.when(i >= nv_ref[0])
    def _():
        zero = jnp.zeros(o_ref.shape, F32)
        o_ref[...] = _pack2(zero, zero)


def _moe_down(tile_expert, n_valid, act, w2b, b2):
    r = act.shape[0]
    tm = MOE_TM
    nt = r // tm
    grid_spec = pltpu.PrefetchScalarGridSpec(
        num_scalar_prefetch=2,
        grid=(nt,),
        in_specs=[pl.BlockSpec((tm, D_EXPERT), lambda i, te, nv: (i, 0)),
                  pl.BlockSpec((None, D_EXPERT, D_MODEL), lambda i, te, nv: (te[i], 0, 0)),
                  pl.BlockSpec((None, 1, D_MODEL), lambda i, te, nv: (te[i], 0, 0))],
        out_specs=pl.BlockSpec((tm, HALF), lambda i, te, nv: (i, 0)),
        scratch_shapes=[pltpu.VMEM((D_EXPERT, D_MODEL), BF16)],
    )
    return pl.pallas_call(
        _moe_down_body, grid_spec=grid_spec,
        out_shape=jax.ShapeDtypeStruct((r, HALF), U32),
        compiler_params=_cparams(1),
        name="moe_down",
    )(tile_expert, n_valid, act, w2b, b2)


def _combine_body(idx_ref, idx_next_ref, y_hbm, h1_ref, g_ref, o_ref, buf_ref, sem, *, tc):
    n = TOP_K * tc
    i = pl.program_id(0)
    slot = i % 2

    def issue(ids_ref, s):
        def start(g, carry):
            for u in range(DMA_UNROLL):
                a = g * DMA_UNROLL + u
                _row_copy(y_hbm, ids_ref[0, 0, a], buf_ref.at[s], a, sem.at[s]).start(priority=u % 2)
            return carry
        lax.fori_loop(0, n // DMA_UNROLL, start, 0)

    @pl.when(i == 0)
    def _():
        issue(idx_ref, 0)

    @pl.when(i + 1 < pl.num_programs(0))
    def _():
        issue(idx_next_ref, 1 - slot)

    def wait(g, carry):
        for u in range(DMA_UNROLL):
            a = g * DMA_UNROLL + u
            _row_copy(y_hbm, 0, buf_ref.at[slot], a, sem.at[slot]).wait()
        return carry

    lax.fori_loop(0, n // DMA_UNROLL, wait, 0)

    g = g_ref[...]
    lo = h1_ref[:, :HALF]
    hi = h1_ref[:, HALF:]
    for kk in range(TOP_K):
        w = buf_ref[slot, kk * tc:(kk + 1) * tc, :]
        gk = g[:, kk:kk + 1]
        lo = lo + gk * _unpack2(w, 0)
        hi = hi + gk * _unpack2(w, 1)
    o_ref[:, :HALF] = lo
    o_ref[:, HALF:] = hi


def _combine(pos_km, y_rows, h1, gates):
    t, d = h1.shape
    tc = 128
    nsteps = t // tc
    return pl.pallas_call(
        functools.partial(_combine_body, tc=tc),
        grid=(nsteps,),
        in_specs=[pl.BlockSpec((1, 1, TOP_K * tc), lambda i: (i, 0, 0), memory_space=pltpu.SMEM),
                  pl.BlockSpec((1, 1, TOP_K * tc), lambda i: (jnp.minimum(i + 1, nsteps - 1), 0, 0),
                               memory_space=pltpu.SMEM),
                  pl.BlockSpec(memory_space=pl.ANY),
                  pl.BlockSpec((tc, d), lambda i: (i, 0)),
                  pl.BlockSpec((tc, LANES), lambda i: (i, 0))],
        out_specs=pl.BlockSpec((tc, d), lambda i: (i, 0)),
        out_shape=jax.ShapeDtypeStruct((t, d), F32),
        scratch_shapes=[pltpu.VMEM((2, TOP_K * tc, HALF), U32), pltpu.SemaphoreType.DMA((2,))],
        compiler_params=_cparams(1),
        name="moe_combine",
    )(pos_km, pos_km, y_rows, h1, gates)


def _route(ti, counts, tm):
    t = ti.shape[0]
    n_assign = t * TOP_K
    n_rows = n_assign + N_EXPERTS * tm
    nt = n_rows // tm
    top_e = ti[:, :TOP_K]
    rank = ti[:, TOP_K:2 * TOP_K]
    counts = counts[0, :N_EXPERTS]
    tiles_per = (counts + (tm - 1)) // tm
    tile_end = jnp.cumsum(tiles_per)
    pstart = (tile_end - tiles_per) * tm
    experts = jnp.arange(N_EXPERTS, dtype=I32)
    onehot = top_e[:, :, None] == experts[None, None, :]
    pos = jnp.sum(jnp.where(onehot, pstart[None, None, :], 0), axis=-1) + rank
    n_valid = tile_end[N_EXPERTS - 1]
    tile_ids = jnp.minimum(jnp.arange(nt, dtype=I32), n_valid - 1)
    tile_expert = jnp.sum((tile_end[None, :] <= tile_ids[:, None]).astype(I32), axis=1)
    tile_expert = jnp.minimum(tile_expert, N_EXPERTS - 1)
    return pos, tile_expert, n_valid.reshape(1), n_rows


def _layer(h, cos_t, sin_t, lam_init, norm1_g, w_in, conv_w, conv_b, dt_bias, a_log, d_skip, ssm_norm_g,
           q_norm_g, k_norm_g, lq1, lk1, lq2, lk2, subln_g, w_out, norm2_g, w_router, b_router, w1, b1, w2, b2):
    bsz, seq, d = h.shape
    t = bsz * seq
    x2 = h.reshape(t, d)

    o_z = 0
    o_xbc = D_SSM
    o_dt = o_xbc + D_SSM + 2 * BC_W
    o_q = o_dt + SSM_HEADS
    o_k = o_q + D_QK
    o_v = o_k + D_QK
    w_p = jnp.concatenate([
        w_in[:, o_z:o_z + D_SSM], w_in[:, o_xbc:o_xbc + D_SSM],
        w_in[:, o_q:o_q + D_QK], w_in[:, o_k:o_k + D_QK], w_in[:, o_v:o_v + D_V],
        w_in[:, o_xbc + D_SSM:o_xbc + D_SSM + 2 * BC_W],
        jnp.pad(w_in[:, o_dt:o_dt + SSM_HEADS], ((0, 0), (0, LANES - SSM_HEADS + PROJ_PAD)))],
        axis=1).astype(BF16)

    proj = _in_proj(x2, norm1_g[None, :], w_p)
    y_ssm = _ssd(proj, bsz, seq, conv_w, conv_b, dt_bias, a_log, d_skip, ssm_norm_g)

    reps = D_QK // DIFF_HEAD_DIM
    qr, kr, vb = _qk_prep(proj, cos_t, sin_t, jnp.tile(q_norm_g, reps)[None, :], jnp.tile(k_norm_g, reps)[None, :])
    lam = (jnp.exp(jnp.sum(lq1 * lk1)) - jnp.exp(jnp.sum(lq2 * lk2)) + lam_init).reshape(1).astype(F32)
    y_attn = _attention(qr, kr, vb, lam, subln_g, bsz, seq, 1.0 - lam_init)

    w_ob = w_out.astype(BF16)
    w_r = jnp.pad(w_router, ((0, 0), (0, LANES - N_EXPERTS)))
    b_r = jnp.pad(b_router, (0, LANES - N_EXPERTS))[None, :]
    h1, h2p, ti, tg, counts = _out_proj_router(x2, y_ssm, y_attn, w_ob[:D_SSM], w_ob[D_SSM:], norm2_g[None, :],
                                               w_r, b_r)

    pos, tile_expert, n_valid, n_rows = _route(ti, counts, MOE_TM)
    xs = _dispatch_rows(h2p, pos, n_rows)
    act = _moe_up(tile_expert, n_valid, xs, w1, b1[:, None, 0::2], b1[:, None, 1::2])
    y_rows = _moe_down(tile_expert, n_valid, act, w2, b2[:, None, :])

    tc = 128
    pos_km = pos.reshape(t // tc, tc, TOP_K).transpose(0, 2, 1).reshape(t // tc, 1, TOP_K * tc)
    out = _combine(pos_km, y_rows, h1, tg)
    return out.reshape(bsz, seq, d)


def kernel(x, positions, norm1_g, w_in, conv_w, conv_b, dt_bias, a_log, d_skip, ssm_norm_g, q_norm_g, k_norm_g,
           lambda_q1, lambda_k1, lambda_q2, lambda_k2, subln_g, w_out, norm2_g, w_router, b_router, w1, b1, w2, b2):
    bsz, seq, _ = x.shape
    inv = ROPE_THETA ** (-jnp.arange(0, DIFF_HEAD_DIM, 2, dtype=F32) / DIFF_HEAD_DIM)
    ang = positions.astype(F32).reshape(bsz * seq, 1) * inv[None, :]
    cos, sin = jnp.cos(ang), jnp.sin(ang)
    cos_t = jnp.tile(cos, (1, 2 * LANES // DIFF_HEAD_DIM))
    sin_t = jnp.tile(jnp.concatenate([-sin, sin], axis=1), (1, LANES // DIFF_HEAD_DIM))

    h = x
    for layer in range(norm1_g.shape[0]):
        lam_init = 0.8 - 0.6 * math.exp(-0.3 * layer)
        h = _layer(h, cos_t, sin_t, lam_init, norm1_g[layer], w_in[layer], conv_w[layer], conv_b[layer],
                   dt_bias[layer], a_log[layer], d_skip[layer], ssm_norm_g[layer], q_norm_g[layer],
                   k_norm_g[layer], lambda_q1[layer], lambda_k1[layer], lambda_q2[layer], lambda_k2[layer],
                   subln_g[layer], w_out[layer], norm2_g[layer], w_router[layer], b_router[layer],
                   w1[layer], b1[layer], w2[layer], b2[layer])
    return h
```

```python
import functools
import math

import jax
import jax.numpy as jnp
from jax import lax
from jax.experimental import pallas as pl
from jax.experimental.pallas import tpu as pltpu

F32 = jnp.float32
BF16 = jnp.bfloat16
U32 = jnp.uint32
I32 = jnp.int32

D_MODEL = 2048
D_SSM = 1024
D_ATTN = 1024
SSM_HEAD_DIM = 64
SSM_HEADS = 16
SSM_GROUPS = 2
SSM_STATE = 128
CONV_WIDTH = 4
CHUNK = 128
DIFF_HEAD_DIM = 64
DIFF_V_DIM = 128
DIFF_HEADS = 8
D_QK = 1024
D_V = 1024
ROPE_THETA = 10000.0
N_EXPERTS = 32
TOP_K = 4
D_EXPERT = 2048
SWIGLU_ALPHA = 1.702
SWIGLU_LIMIT = 7.0
NORM_EPS = 1e-5
LANES = 128
SUBLANES = 8
HALF = D_MODEL // 2

BC_W = SSM_GROUPS * SSM_STATE
PROJ_COLS = D_SSM + D_SSM + D_QK + D_QK + D_V + 2 * BC_W + LANES
INPROJ_TN = 1536
PROJ_PAD = -PROJ_COLS % INPROJ_TN

MOE_TM = 512
VMEM_LIMIT = 56 * 1024 * 1024


def _cparams(n_axes, vmem=VMEM_LIMIT):
    return pltpu.CompilerParams(dimension_semantics=("arbitrary",) * n_axes, vmem_limit_bytes=vmem)


def _sigmoid(v):
    return 1.0 / (1.0 + jnp.exp(-v))


def _pack2(lo, hi):
    return pltpu.pack_elementwise([lo, hi], packed_dtype=BF16)


def _unpack2(w, index):
    return pltpu.unpack_elementwise(w, index=index, packed_dtype=BF16, unpacked_dtype=F32)


def _inproj_body(x_ref, g_ref, w_ref, o_ref, u_ref):
    @pl.when(pl.program_id(1) == 0)
    def _():
        x = x_ref[...]
        ms = jnp.mean(x * x, axis=-1, keepdims=True)
        u_ref[...] = (x * lax.rsqrt(ms + NORM_EPS) * g_ref[...]).astype(BF16)

    o_ref[...] = jnp.dot(u_ref[...], w_ref[...], preferred_element_type=F32)


def _in_proj(x2, g, w_p):
    t, d = x2.shape
    ncol = w_p.shape[1]
    tm, tn = 1024, INPROJ_TN
    return pl.pallas_call(
        _inproj_body,
        grid=(t // tm, ncol // tn),
        in_specs=[pl.BlockSpec((tm, d), lambda i, j: (i, 0)),
                  pl.BlockSpec((1, d), lambda i, j: (0, 0)),
                  pl.BlockSpec((d, tn), lambda i, j: (0, j))],
        out_specs=pl.BlockSpec((tm, tn), lambda i, j: (i, j)),
        out_shape=jax.ShapeDtypeStruct((t, ncol), F32),
        scratch_shapes=[pltpu.VMEM((tm, d), BF16)],
        compiler_params=_cparams(2),
        name="in_proj",
    )(x2, g, w_p)


def _ssd_body(z_ref, x_ref, b_ref, c_ref, dt_ref, cwx_ref, cwb_ref, cwc_ref, cbx_ref, cbb_ref, cbc_ref,
              dtb_ref, alog_ref, dsk_ref, ng_ref, e_ref, y_ref, tx_ref, tb_ref, tc_ref, st_ref):
    @pl.when(pl.program_id(1) == 0)
    def _():
        tx_ref[...] = jnp.zeros_like(tx_ref)
        tb_ref[...] = jnp.zeros_like(tb_ref)
        tc_ref[...] = jnp.zeros_like(tc_ref)
        st_ref[...] = jnp.zeros_like(st_ref)

    row8 = lax.broadcasted_iota(I32, (SUBLANES, 1), 0)

    def conv_silu(u_ref, t_ref, w_ref, bias_ref):
        u = u_ref[...]
        tail = t_ref[...]
        w = w_ref[...]
        acc = u * w[CONV_WIDTH - 1:CONV_WIDTH, :] + bias_ref[...]
        for s in range(1, CONV_WIDTH):
            ru = pltpu.roll(u, s, 0)
            rt = pltpu.roll(tail, s, 0)
            head = jnp.where(row8 < s, rt, ru[0:SUBLANES, :])
            sh = jnp.concatenate([head, ru[SUBLANES:, :]], axis=0)
            acc = acc + sh * w[CONV_WIDTH - 1 - s:CONV_WIDTH - s, :]
        t_ref[...] = u[CHUNK - SUBLANES:CHUNK, :]
        return acc * _sigmoid(acc)

    xs = conv_silu(x_ref, tx_ref, cwx_ref, cbx_ref)
    bm = conv_silu(b_ref, tb_ref, cwb_ref, cbb_ref)
    cm = conv_silu(c_ref, tc_ref, cwc_ref, cbc_ref)

    dtr = dt_ref[...] + dtb_ref[...]
    dt = jnp.maximum(dtr, 0.0) + jnp.log1p(jnp.exp(-jnp.abs(dtr)))
    da = dt * (-jnp.exp(alog_ref[...]))

    rowi = lax.broadcasted_iota(I32, (CHUNK, 1), 0)
    acs = da
    k = 1
    while k < CHUNK:
        acs = acs + jnp.where(rowi >= k, pltpu.roll(acs, k, 0), 0.0)
        k *= 2
    acs_t = acs.T

    both = jnp.concatenate([dt, acs], axis=0)
    ex = None
    for _ in range(3):
        part = both.astype(BF16)
        term = jnp.dot(part, e_ref[...], preferred_element_type=F32)
        ex = term if ex is None else ex + term
        both = both - part.astype(F32)
    dt_f = ex[:CHUNK]
    a_f = ex[CHUNK:]
    a_last = a_f[CHUNK - 1:CHUNK, :]
    expa = jnp.exp(a_f)
    dstate = jnp.exp(a_last - a_f)
    cdec = jnp.exp(a_last)

    xdt = xs * dt_f
    xdt_b = xdt.astype(BF16)
    xds_b = (xdt * dstate).astype(BF16)

    coli = lax.broadcasted_iota(I32, (1, CHUNK), 1)
    tril = rowi >= coli
    lane = lax.broadcasted_iota(I32, (1, LANES), 1)
    hpg = SSM_HEADS // SSM_GROUPS
    gw = hpg * SSM_HEAD_DIM
    y_groups = []
    for g in range(SSM_GROUPS):
        bm_g = bm[:, g * SSM_STATE:(g + 1) * SSM_STATE]
        cm_g = cm[:, g * SSM_STATE:(g + 1) * SSM_STATE]
        cm_b = cm_g.astype(BF16)
        cb = lax.dot_general(cm_b, bm_g.astype(BF16), (((1,), (1,)), ((), ())),
                             preferred_element_type=F32)
        parts = []
        for pp in range(hpg // 2):
            p = g * (hpg // 2) + pp
            xp = xdt_b[:, p * LANES:(p + 1) * LANES]
            ys = []
            for hh in range(2):
                h = 2 * p + hh
                seg = acs[:, h:h + 1] - acs_t[h:h + 1, :]
                lm = jnp.where(tril, jnp.exp(seg), 0.0)
                ys.append(jnp.dot((cb * lm).astype(BF16), xp, preferred_element_type=F32))
            parts.append(jnp.where(lane < SSM_HEAD_DIM, ys[0], ys[1]))
        y_diag = jnp.concatenate(parts, axis=1)
        st_g = st_ref[:, g * gw:(g + 1) * gw]
        y_off = jnp.dot(cm_b, st_g.astype(BF16), preferred_element_type=F32) * expa[:, g * gw:(g + 1) * gw]
        new = jnp.dot(bm_g.T.astype(BF16), xds_b[:, g * gw:(g + 1) * gw], preferred_element_type=F32)
        st_ref[:, g * gw:(g + 1) * gw] = st_g * cdec[:, g * gw:(g + 1) * gw] + new
        y_groups.append(y_diag + y_off)

    zz = z_ref[...]
    gate = zz * _sigmoid(zz)
    dsk = dsk_ref[...]
    ng = ng_ref[...]
    outs = []
    for g in range(SSM_GROUPS):
        sl = slice(g * gw, (g + 1) * gw)
        yg = (y_groups[g] + xs[:, sl] * dsk[:, sl]) * gate[:, sl]
        ms = jnp.mean(yg * yg, axis=-1, keepdims=True)
        outs.append(yg * lax.rsqrt(ms + NORM_EPS) * ng[:, sl])
    y_ref[...] = jnp.concatenate(outs, axis=1).astype(BF16)


def _ssd(proj, bsz, seq, cw, cb, dt_bias, a_log, d_skip, norm_g):
    t = bsz * seq
    nc = seq // CHUNK
    cwx, cwb, cwc = cw[:, :D_SSM], cw[:, D_SSM:D_SSM + BC_W], cw[:, D_SSM + BC_W:]
    cb = cb[None, :]
    cbx, cbb, cbc = cb[:, :D_SSM], cb[:, D_SSM:D_SSM + BC_W], cb[:, D_SSM + BC_W:]
    pad = LANES - SSM_HEADS
    dtb = jnp.pad(dt_bias, (0, pad))[None, :]
    alog = jnp.pad(a_log, (0, pad))[None, :]
    dsk = jnp.repeat(d_skip, SSM_HEAD_DIM)[None, :]
    expand = (jnp.arange(D_SSM)[None, :] // SSM_HEAD_DIM == jnp.arange(LANES)[:, None]).astype(BF16)

    def row(b, c):
        return b * nc + c

    def full(shape):
        return pl.BlockSpec(shape, lambda b, c: (0, 0))

    bc0 = (2 * D_SSM + 2 * D_QK + D_V) // BC_W
    dt0 = (PROJ_COLS - LANES) // LANES
    return pl.pallas_call(
        _ssd_body,
        grid=(bsz, nc),
        in_specs=[pl.BlockSpec((CHUNK, D_SSM), lambda b, c: (row(b, c), 0)),
                  pl.BlockSpec((CHUNK, D_SSM), lambda b, c: (row(b, c), 1)),
                  pl.BlockSpec((CHUNK, BC_W), lambda b, c: (row(b, c), bc0)),
                  pl.BlockSpec((CHUNK, BC_W), lambda b, c: (row(b, c), bc0 + 1)),
                  pl.BlockSpec((CHUNK, LANES), lambda b, c: (row(b, c), dt0)),
                  full((CONV_WIDTH, D_SSM)), full((CONV_WIDTH, BC_W)), full((CONV_WIDTH, BC_W)),
                  full((1, D_SSM)), full((1, BC_W)), full((1, BC_W)),
                  full((1, LANES)), full((1, LANES)), full((1, D_SSM)), full((1, D_SSM)),
                  full((LANES, D_SSM))],
        out_specs=pl.BlockSpec((CHUNK, D_SSM), lambda b, c: (row(b, c), 0)),
        out_shape=jax.ShapeDtypeStruct((t, D_SSM), BF16),
        scratch_shapes=[pltpu.VMEM((SUBLANES, D_SSM), F32), pltpu.VMEM((SUBLANES, BC_W), F32),
                        pltpu.VMEM((SUBLANES, BC_W), F32), pltpu.VMEM((SSM_STATE, D_SSM), F32)],
        compiler_params=_cparams(2),
        name="ssd",
    )(proj, proj, proj, proj, proj, cwx, cwb, cwc, cbx, cbb, cbc, dtb, alog, dsk, norm_g[None, :], expand)


def _qkprep_body(q_ref, k_ref, v_ref, cos_ref, sin_ref, gq_ref, gk_ref, gm_ref, qo_ref, ko_ref, vo_ref):
    reps = D_QK // LANES
    cos = jnp.tile(cos_ref[...], (1, reps))
    sin = jnp.tile(sin_ref[...], (1, reps))
    lane = lax.broadcasted_iota(I32, (1, D_QK), 1)
    first = (lane & (DIFF_HEAD_DIM - 1)) < DIFF_HEAD_DIM // 2
    gm = gm_ref[...]

    def prep(t, g, scale):
        sq = t * t
        hi = sq.astype(BF16)
        lo = (sq - hi.astype(F32)).astype(BF16)
        ss = jnp.dot(hi, gm, preferred_element_type=F32) + jnp.dot(lo, gm, preferred_element_type=F32)
        tn = t * lax.rsqrt(ss * (1.0 / DIFF_HEAD_DIM) + NORM_EPS) * g
        half = DIFF_HEAD_DIM // 2
        rot = jnp.where(first, pltpu.roll(tn, D_QK - half, 1), pltpu.roll(tn, half, 1))
        return (tn * cos + rot * sin) * scale

    qo_ref[...] = prep(q_ref[...], gq_ref[...], DIFF_HEAD_DIM ** -0.5 * math.log2(math.e)).astype(BF16)
    ko_ref[...] = prep(k_ref[...], gk_ref[...], 1.0).astype(BF16)
    vo_ref[...] = v_ref[...].astype(BF16)


def _qk_prep(proj, cos_t, sin_t, gq, gk):
    t = proj.shape[0]
    tm = 256
    gsum = (jnp.arange(D_QK)[:, None] // DIFF_HEAD_DIM == jnp.arange(D_QK)[None, :] // DIFF_HEAD_DIM).astype(BF16)
    out = jax.ShapeDtypeStruct((t, D_QK), BF16)
    return pl.pallas_call(
        _qkprep_body,
        grid=(t // tm,),
        in_specs=[pl.BlockSpec((tm, D_QK), lambda i: (i, 2)),
                  pl.BlockSpec((tm, D_QK), lambda i: (i, 3)),
                  pl.BlockSpec((tm, D_V), lambda i: (i, 4)),
                  pl.BlockSpec((tm, LANES), lambda i: (i, 0)),
                  pl.BlockSpec((tm, LANES), lambda i: (i, 0)),
                  pl.BlockSpec((1, D_QK), lambda i: (0, 0)),
                  pl.BlockSpec((1, D_QK), lambda i: (0, 0)),
                  pl.BlockSpec((D_QK, D_QK), lambda i: (0, 0))],
        out_specs=[pl.BlockSpec((tm, D_QK), lambda i: (i, 0))] * 3,
        out_shape=[out, out, out],
        compiler_params=_cparams(1),
        name="qk_prep",
    )(proj, proj, proj, cos_t, sin_t, gq, gk, gsum)


KV_UNROLL = 4


def _attn_body(lam_ref, q_ref, k_ref, v_ref, sg_ref, o_ref, qs_ref, vx_ref, m_ref, acc_ref, *, tq, row_chunk,
               out_scale):
    i = pl.program_id(2)

    @pl.when(i == 0)
    def _():
        vx_ref[:, 0:LANES] = v_ref[...]
        vx_ref[:, LANES:2 * LANES] = jnp.ones((vx_ref.shape[0], LANES), BF16)

    q = q_ref[...]
    lane = lax.broadcasted_iota(I32, (1, LANES), 1)
    zero = jnp.zeros_like(q)
    qs_ref[0:tq, :] = jnp.where(lane < DIFF_HEAD_DIM, q, zero)
    qs_ref[tq:2 * tq, :] = jnp.where(lane >= DIFF_HEAD_DIM, q, zero)
    m_ref[...] = jnp.full_like(m_ref, -jnp.inf)
    acc_ref[...] = jnp.zeros_like(acc_ref)

    def step(j, masked):
        start = pl.multiple_of(j * tq, tq)
        k = k_ref[pl.ds(start, tq), :]
        vx = vx_ref[pl.ds(start, tq), :]
        for rc in range(2 * tq // row_chunk):
            rows = slice(rc * row_chunk, (rc + 1) * row_chunk)
            s = lax.dot_general(qs_ref[rows, :], k, (((1,), (1,)), ((), ())), preferred_element_type=F32)
            if masked:
                r = lax.broadcasted_iota(I32, (row_chunk, tq), 0) + (rc * row_chunk) % tq
                c = lax.broadcasted_iota(I32, (row_chunk, tq), 1)
                s = jnp.where(c <= r, s, -jnp.inf)
            m_prev = m_ref[rows, :]
            m_new = jnp.maximum(m_prev, jnp.max(s, axis=1, keepdims=True))
            alpha = jnp.exp2(m_prev - m_new)
            p = jnp.exp2(s - jnp.tile(m_new, (1, tq // LANES)))
            pv = jnp.dot(p.astype(BF16), vx, preferred_element_type=F32)
            acc_ref[rows, :] = jnp.tile(alpha, (1, 2)) * acc_ref[rows, :] + pv
            m_ref[rows, :] = m_new

    def off_diagonal_group(jj, carry):
        for u in range(KV_UNROLL):
            step(KV_UNROLL * jj + u, False)
        return carry

    lax.fori_loop(0, i // KV_UNROLL, off_diagonal_group, 0)
    done = (i // KV_UNROLL) * KV_UNROLL
    for rem in range(KV_UNROLL):
        @pl.when(i - done == rem)
        def _(rem=rem):
            for u in range(rem):
                step(done + u, False)
            step(i, True)

    acc = acc_ref[...]
    o = acc[:, 0:LANES] / acc[:, LANES:2 * LANES]
    d = o[0:tq, :] - lam_ref[0] * o[tq:2 * tq, :]
    ms = jnp.mean(d * d, axis=-1, keepdims=True)
    o_ref[...] = (d * lax.rsqrt(ms + NORM_EPS) * sg_ref[...] * out_scale).astype(BF16)


def _attention(qr, kr, vb, lam, subln_g, bsz, seq, out_scale):
    t = bsz * seq
    tq = 512
    nq = seq // tq
    body = functools.partial(_attn_body, tq=tq, row_chunk=128, out_scale=out_scale)
    grid_spec = pltpu.PrefetchScalarGridSpec(
        num_scalar_prefetch=1,
        grid=(bsz, DIFF_HEADS, nq),
        in_specs=[pl.BlockSpec((tq, LANES), lambda b, h, i, lam: (b * nq + i, h)),
                  pl.BlockSpec((seq, LANES), lambda b, h, i, lam: (b, h)),
                  pl.BlockSpec((seq, LANES), lambda b, h, i, lam: (b, h)),
                  pl.BlockSpec((1, LANES), lambda b, h, i, lam: (0, 0))],
        out_specs=pl.BlockSpec((tq, LANES), lambda b, h, i, lam: (b * nq + i, h)),
        scratch_shapes=[pltpu.VMEM((2 * tq, LANES), BF16), pltpu.VMEM((seq, 2 * LANES), BF16),
                        pltpu.VMEM((2 * tq, LANES), F32), pltpu.VMEM((2 * tq, 2 * LANES), F32)],
    )
    return pl.pallas_call(
        body, grid_spec=grid_spec,
        out_shape=jax.ShapeDtypeStruct((t, D_ATTN), BF16),
        compiler_params=_cparams(3),
        name="diff_attention",
    )(lam, qr, kr, vb, subln_g[None, :])


def _outproj_body(x_ref, ys_ref, ya_ref, wt_ref, wb_ref, g2_ref, wrh_ref, wrl_ref, br_ref, ltri_ref,
                  h1_ref, hp_ref, ti_ref, tg_ref, cnt_out_ref, cnt_ref):
    h1 = (x_ref[...] + jnp.dot(ys_ref[...], wt_ref[...], preferred_element_type=F32)
          + jnp.dot(ya_ref[...], wb_ref[...], preferred_element_type=F32))
    h1_ref[...] = h1
    ms = jnp.mean(h1 * h1, axis=-1, keepdims=True)
    h2 = h1 * lax.rsqrt(ms + NORM_EPS) * g2_ref[...]
    hp_ref[...] = _pack2(h2[:, :HALF], h2[:, HALF:])

    h2_hi = h2.astype(BF16)
    h2_lo = (h2 - h2_hi.astype(F32)).astype(BF16)
    logits = (jnp.dot(h2_hi, wrh_ref[...], preferred_element_type=F32)
              + jnp.dot(h2_hi, wrl_ref[...], preferred_element_type=F32)
              + jnp.dot(h2_lo, wrh_ref[...], preferred_element_type=F32) + br_ref[...])
    lane = lax.broadcasted_iota(I32, (1, LANES), 1)
    lanef = lane.astype(F32)
    cur = jnp.where(lane < N_EXPERTS, logits, -jnp.inf)
    vals, idxs = [], []
    for _ in range(TOP_K):
        m = jnp.max(cur, axis=1, keepdims=True)
        idx = jnp.min(jnp.where(cur == m, lanef, float(LANES)), axis=1, keepdims=True)
        vals.append(m)
        idxs.append(idx)
        cur = jnp.where(lanef == idx, -jnp.inf, cur)
    es = [jnp.exp(v - vals[0]) for v in vals]
    den = es[0] + es[1] + es[2] + es[3]

    @pl.when(pl.program_id(0) == 0)
    def _():
        cnt_ref[...] = jnp.zeros_like(cnt_ref)

    base = cnt_ref[...]
    ltri = ltri_ref[...]
    ti = jnp.zeros(logits.shape, F32)
    tg = jnp.zeros(logits.shape, F32)
    for kk in range(TOP_K):
        oh = lanef == idxs[kk]
        ohf = jnp.where(oh, 1.0, 0.0)
        pre = jnp.dot(ltri, ohf.astype(BF16), preferred_element_type=F32)
        rank = jnp.sum(jnp.where(oh, pre + base, 0.0), axis=1, keepdims=True)
        base = base + jnp.sum(ohf, axis=0, keepdims=True)
        ti = jnp.where(lane == kk, idxs[kk], ti)
        ti = jnp.where(lane == TOP_K + kk, rank, ti)
        tg = jnp.where(lane == kk, es[kk] / den, tg)
    cnt_ref[...] = base
    ti_ref[...] = ti.astype(I32)
    tg_ref[...] = tg
    cnt_out_ref[...] = jnp.broadcast_to(base, cnt_out_ref.shape).astype(I32)


def _out_proj_router(x2, y_ssm, y_attn, w_top, w_bot, g2, w_r, b_r):
    t, d = x2.shape
    tm = 256
    w_rh = w_r.astype(BF16)
    w_rl = (w_r - w_rh.astype(F32)).astype(BF16)
    ltri = (jnp.arange(tm)[:, None] > jnp.arange(tm)[None, :]).astype(BF16)

    def full(shape):
        return pl.BlockSpec(shape, lambda i: (0, 0))

    return pl.pallas_call(
        _outproj_body,
        grid=(t // tm,),
        in_specs=[pl.BlockSpec((tm, d), lambda i: (i, 0)),
                  pl.BlockSpec((tm, D_SSM), lambda i: (i, 0)),
                  pl.BlockSpec((tm, D_ATTN), lambda i: (i, 0)),
                  full((D_SSM, d)), full((D_ATTN, d)), full((1, d)), full((d, LANES)), full((d, LANES)),
                  full((1, LANES)), full((tm, tm))],
        out_specs=[pl.BlockSpec((tm, d), lambda i: (i, 0)),
                   pl.BlockSpec((tm, HALF), lambda i: (i, 0)),
                   pl.BlockSpec((tm, LANES), lambda i: (i, 0)),
                   pl.BlockSpec((tm, LANES), lambda i: (i, 0)),
                   full((SUBLANES, LANES))],
        out_shape=[jax.ShapeDtypeStruct((t, d), F32), jax.ShapeDtypeStruct((t, HALF), U32),
                   jax.ShapeDtypeStruct((t, LANES), I32), jax.ShapeDtypeStruct((t, LANES), F32),
                   jax.ShapeDtypeStruct((SUBLANES, LANES), I32)],
        scratch_shapes=[pltpu.VMEM((1, LANES), F32)],
        compiler_params=_cparams(1),
        name="out_proj_router",
    )(x2, y_ssm, y_attn, w_top, w_bot, g2, w_rh, w_rl, b_r, ltri)


def _row_copy(src_hbm, src_row, dst_ref, dst_row, sem):
    return pltpu.make_async_copy(src_hbm.at[pl.ds(src_row, 1), :], dst_ref.at[pl.ds(dst_row, 1), :], sem)


DMA_UNROLL = 8


def _zero_rows_body(o_ref):
    zero = jnp.zeros(o_ref.shape, F32)
    o_ref[...] = _pack2(zero, zero)


def _zero_rows(n_rows, rows_per_step):
    return pl.pallas_call(
        _zero_rows_body,
        grid=(n_rows // rows_per_step,),
        out_specs=pl.BlockSpec((rows_per_step, HALF), lambda i: (i, 0)),
        out_shape=jax.ShapeDtypeStruct((n_rows, HALF), U32),
        compiler_params=_cparams(1),
        name="zero_rows",
    )()


def _dispatch_body(pos_ref, x_ref, init_hbm, dst_hbm, sem, *, tokens):
    del init_hbm
    n = TOP_K * tokens

    def start(g, carry):
        for u in range(DMA_UNROLL):
            a = g * DMA_UNROLL + u
            t = g * (DMA_UNROLL // TOP_K) + u // TOP_K
            _row_copy(x_ref, t, dst_hbm, pos_ref[0, 0, a], sem).start(priority=u % 2)
        return carry

    def wait(g, carry):
        for u in range(DMA_UNROLL):
            t = g * (DMA_UNROLL // TOP_K) + u // TOP_K
            _row_copy(x_ref, t, dst_hbm, 0, sem).wait()
        return carry

    lax.fori_loop(0, n // DMA_UNROLL, start, 0)
    lax.fori_loop(0, n // DMA_UNROLL, wait, 0)


def _dispatch_rows(src, pos, n_rows):
    t, w = src.shape
    tokens = 256
    nsteps = t // tokens
    return pl.pallas_call(
        functools.partial(_dispatch_body, tokens=tokens),
        grid=(nsteps,),
        in_specs=[pl.BlockSpec((1, 1, TOP_K * tokens), lambda i: (i, 0, 0), memory_space=pltpu.SMEM),
                  pl.BlockSpec((tokens, w), lambda i: (i, 0)),
                  pl.BlockSpec(memory_space=pl.ANY)],
        out_specs=pl.BlockSpec(memory_space=pl.ANY),
        out_shape=jax.ShapeDtypeStruct((n_rows, w), src.dtype),
        scratch_shapes=[pltpu.SemaphoreType.DMA(())],
        input_output_aliases={2: 0},
        compiler_params=_cparams(1),
        name="dispatch_rows",
    )(pos.reshape(nsteps, 1, TOP_K * tokens), src, _zero_rows(n_rows, 2048))


MXU_W = 256


def _moe_up_body(te_ref, nv_ref, x_ref, w_ref, p_ref, bg_ref, bl_ref, o_ref, wg_ref, wl_ref):
    i = pl.program_id(1)
    new_expert = jnp.logical_or(i == 0, te_ref[i] != te_ref[jnp.maximum(i - 1, 0)])

    @pl.when(new_expert)
    def _():
        half = MXU_W // 2
        for c in range(w_ref.shape[1] // MXU_W):
            wc = w_ref[:, c * MXU_W:(c + 1) * MXU_W].astype(BF16)
            r = jnp.dot(wc, p_ref[...], preferred_element_type=F32)
            wg_ref[:, c * half:(c + 1) * half] = r[:, :half].astype(BF16)
            wl_ref[:, c * half:(c + 1) * half] = r[:, half:].astype(BF16)

    @pl.when(i < nv_ref[0])
    def _():
        w = x_ref[...]
        lo = _unpack2(w, 0).astype(BF16)
        hi = _unpack2(w, 1).astype(BF16)

        def up(ws_ref, b_ref):
            return (jnp.dot(lo, ws_ref[0:HALF, :], preferred_element_type=F32)
                    + jnp.dot(hi, ws_ref[HALF:, :], preferred_element_type=F32) + b_ref[...])

        glu = jnp.minimum(up(wg_ref, bg_ref), SWIGLU_LIMIT)
        lin = jnp.clip(up(wl_ref, bl_ref), -SWIGLU_LIMIT, SWIGLU_LIMIT)
        o_ref[...] = (glu * _sigmoid(SWIGLU_ALPHA * glu) * (lin + 1.0)).astype(BF16)

    @pl.when(i >= nv_ref[0])
    def _():
        o_ref[...] = jnp.zeros_like(o_ref)


def _moe_up(tile_expert, n_valid, xs, w1, b1g, b1l):
    r = xs.shape[0]
    tm, tn = MOE_TM, 1024
    nt = r // tm
    j = jnp.arange(MXU_W)
    perm = (j[None, :] == jnp.where(j % 2 == 0, j // 2, MXU_W // 2 + j // 2)[:, None]).astype(BF16)
    grid_spec = pltpu.PrefetchScalarGridSpec(
        num_scalar_prefetch=2,
        grid=(D_EXPERT // tn, nt),
        in_specs=[pl.BlockSpec((tm, HALF), lambda n, i, te, nv: (i, 0)),
                  pl.BlockSpec((None, D_MODEL, 2 * tn), lambda n, i, te, nv: (te[i], 0, n)),
                  pl.BlockSpec((MXU_W, MXU_W), lambda n, i, te, nv: (0, 0)),
                  pl.BlockSpec((None, 1, tn), lambda n, i, te, nv: (te[i], 0, n)),
                  pl.BlockSpec((None, 1, tn), lambda n, i, te, nv: (te[i], 0, n))],
        out_specs=pl.BlockSpec((tm, tn), lambda n, i, te, nv: (i, n)),
        scratch_shapes=[pltpu.VMEM((D_MODEL, tn), BF16), pltpu.VMEM((D_MODEL, tn), BF16)],
    )
    return pl.pallas_call(
        _moe_up_body, grid_spec=grid_spec,
        out_shape=jax.ShapeDtypeStruct((r, D_EXPERT), BF16),
        compiler_params=_cparams(2),
        name="moe_up",
    )(tile_expert, n_valid, xs, w1, perm, b1g, b1l)


def _moe_down_body(te_ref, nv_ref, a_ref, w_ref, b_ref, o_ref, wb_ref):
    i = pl.program_id(0)
    new_expert = jnp.logical_or(i == 0, te_ref[i] != te_ref[jnp.maximum(i - 1, 0)])

    @pl.when(new_expert)
    def _():
        wb_ref[...] = w_ref[...].astype(BF16)

    @pl.when(i < nv_ref[0])
    def _():
        y = jnp.dot(a_ref[...], wb_ref[...], preferred_element_type=F32) + b_ref[...]
        o_ref[...] = _pack2(y[:, :HALF], y[:, HALF:])

    @pl.when(i >= nv_ref[0])
    def _():
        zero = jnp.zeros(o_ref.shape, F32)
        o_ref[...] = _pack2(zero, zero)


def _moe_down(tile_expert, n_valid, act, w2b, b2):
    r = act.shape[0]
    tm = MOE_TM
    nt = r // tm
    grid_spec = pltpu.PrefetchScalarGridSpec(
        num_scalar_prefetch=2,
        grid=(nt,),
        in_specs=[pl.BlockSpec((tm, D_EXPERT), lambda i, te, nv: (i, 0)),
                  pl.BlockSpec((None, D_EXPERT, D_MODEL), lambda i, te, nv: (te[i], 0, 0)),
                  pl.BlockSpec((None, 1, D_MODEL), lambda i, te, nv: (te[i], 0, 0))],
        out_specs=pl.BlockSpec((tm, HALF), lambda i, te, nv: (i, 0)),
        scratch_shapes=[pltpu.VMEM((D_EXPERT, D_MODEL), BF16)],
    )
    return pl.pallas_call(
        _moe_down_body, grid_spec=grid_spec,
        out_shape=jax.ShapeDtypeStruct((r, HALF), U32),
        compiler_params=_cparams(1),
        name="moe_down",
    )(tile_expert, n_valid, act, w2b, b2)


def _combine_body(idx_ref, idx_next_ref, y_hbm, h1_ref, g_ref, o_ref, buf_ref, sem, *, tc):
    n = TOP_K * tc
    i = pl.program_id(0)
    slot = i % 2

    def issue(ids_ref, s):
        def start(g, carry):
            for u in range(DMA_UNROLL):
                a = g * DMA_UNROLL + u
                _row_copy(y_hbm, ids_ref[0, 0, a], buf_ref.at[s], a, sem.at[s]).start(priority=u % 2)
            return carry
        lax.fori_loop(0, n // DMA_UNROLL, start, 0)

    @pl.when(i == 0)
    def _():
        issue(idx_ref, 0)

    @pl.when(i + 1 < pl.num_programs(0))
    def _():
        issue(idx_next_ref, 1 - slot)

    def wait(g, carry):
        for u in range(DMA_UNROLL):
            a = g * DMA_UNROLL + u
            _row_copy(y_hbm, 0, buf_ref.at[slot], a, sem.at[slot]).wait()
        return carry

    lax.fori_loop(0, n // DMA_UNROLL, wait, 0)

    g = g_ref[...]
    lo = h1_ref[:, :HALF]
    hi = h1_ref[:, HALF:]
    for kk in range(TOP_K):
        w = buf_ref[slot, kk * tc:(kk + 1) * tc, :]
        gk = g[:, kk:kk + 1]
        lo = lo + gk * _unpack2(w, 0)
        hi = hi + gk * _unpack2(w, 1)
    o_ref[:, :HALF] = lo
    o_ref[:, HALF:] = hi


def _combine(pos_km, y_rows, h1, gates):
    t, d = h1.shape
    tc = 128
    nsteps = t // tc
    return pl.pallas_call(
        functools.partial(_combine_body, tc=tc),
        grid=(nsteps,),
        in_specs=[pl.BlockSpec((1, 1, TOP_K * tc), lambda i: (i, 0, 0), memory_space=pltpu.SMEM),
                  pl.BlockSpec((1, 1, TOP_K * tc), lambda i: (jnp.minimum(i + 1, nsteps - 1), 0, 0),
                               memory_space=pltpu.SMEM),
                  pl.BlockSpec(memory_space=pl.ANY),
                  pl.BlockSpec((tc, d), lambda i: (i, 0)),
                  pl.BlockSpec((tc, LANES), lambda i: (i, 0))],
        out_specs=pl.BlockSpec((tc, d), lambda i: (i, 0)),
        out_shape=jax.ShapeDtypeStruct((t, d), F32),
        scratch_shapes=[pltpu.VMEM((2, TOP_K * tc, HALF), U32), pltpu.SemaphoreType.DMA((2,))],
        compiler_params=_cparams(1),
        name="moe_combine",
    )(pos_km, pos_km, y_rows, h1, gates)


def _route(ti, counts, tm):
    t = ti.shape[0]
    n_assign = t * TOP_K
    n_rows = n_assign + N_EXPERTS * tm
    nt = n_rows // tm
    top_e = ti[:, :TOP_K]
    rank = ti[:, TOP_K:2 * TOP_K]
    counts = counts[0, :N_EXPERTS]
    tiles_per = (counts + (tm - 1)) // tm
    tile_end = jnp.cumsum(tiles_per)
    pstart = (tile_end - tiles_per) * tm
    experts = jnp.arange(N_EXPERTS, dtype=I32)
    onehot = top_e[:, :, None] == experts[None, None, :]
    pos = jnp.sum(jnp.where(onehot, pstart[None, None, :], 0), axis=-1) + rank
    n_valid = tile_end[N_EXPERTS - 1]
    tile_ids = jnp.minimum(jnp.arange(nt, dtype=I32), n_valid - 1)
    tile_expert = jnp.sum((tile_end[None, :] <= tile_ids[:, None]).astype(I32), axis=1)
    tile_expert = jnp.minimum(tile_expert, N_EXPERTS - 1)
    return pos, tile_expert, n_valid.reshape(1), n_rows


def _layer(h, cos_t, sin_t, lam_init, norm1_g, w_in, conv_w, conv_b, dt_bias, a_log, d_skip, ssm_norm_g,
           q_norm_g, k_norm_g, lq1, lk1, lq2, lk2, subln_g, w_out, norm2_g, w_router, b_router, w1, b1, w2, b2):
    bsz, seq, d = h.shape
    t = bsz * seq
    x2 = h.reshape(t, d)

    o_z = 0
    o_xbc = D_SSM
    o_dt = o_xbc + D_SSM + 2 * BC_W
    o_q = o_dt + SSM_HEADS
    o_k = o_q + D_QK
    o_v = o_k + D_QK
    w_p = jnp.concatenate([
        w_in[:, o_z:o_z + D_SSM], w_in[:, o_xbc:o_xbc + D_SSM],
        w_in[:, o_q:o_q + D_QK], w_in[:, o_k:o_k + D_QK], w_in[:, o_v:o_v + D_V],
        w_in[:, o_xbc + D_SSM:o_xbc + D_SSM + 2 * BC_W],
        jnp.pad(w_in[:, o_dt:o_dt + SSM_HEADS], ((0, 0), (0, LANES - SSM_HEADS + PROJ_PAD)))],
        axis=1).astype(BF16)

    proj = _in_proj(x2, norm1_g[None, :], w_p)
    y_ssm = _ssd(proj, bsz, seq, conv_w, conv_b, dt_bias, a_log, d_skip, ssm_norm_g)

    reps = D_QK // DIFF_HEAD_DIM
    qr, kr, vb = _qk_prep(proj, cos_t, sin_t, jnp.tile(q_norm_g, reps)[None, :], jnp.tile(k_norm_g, reps)[None, :])
    lam = (jnp.exp(jnp.sum(lq1 * lk1)) - jnp.exp(jnp.sum(lq2 * lk2)) + lam_init).reshape(1).astype(F32)
    y_attn = _attention(qr, kr, vb, lam, subln_g, bsz, seq, 1.0 - lam_init)

    w_ob = w_out.astype(BF16)
    w_r = jnp.pad(w_router, ((0, 0), (0, LANES - N_EXPERTS)))
    b_r = jnp.pad(b_router, (0, LANES - N_EXPERTS))[None, :]
    h1, h2p, ti, tg, counts = _out_proj_router(x2, y_ssm, y_attn, w_ob[:D_SSM], w_ob[D_SSM:], norm2_g[None, :],
                                               w_r, b_r)

    pos, tile_expert, n_valid, n_rows = _route(ti, counts, MOE_TM)
    xs = _dispatch_rows(h2p, pos, n_rows)
    act = _moe_up(tile_expert, n_valid, xs, w1, b1[:, None, 0::2], b1[:, None, 1::2])
    y_rows = _moe_down(tile_expert, n_valid, act, w2, b2[:, None, :])

    tc = 128
    pos_km = pos.reshape(t // tc, tc, TOP_K).transpose(0, 2, 1).reshape(t // tc, 1, TOP_K * tc)
    out = _combine(pos_km, y_rows, h1, tg)
    return out.reshape(bsz, seq, d)


def kernel(x, positions, norm1_g, w_in, conv_w, conv_b, dt_bias, a_log, d_skip, ssm_norm_g, q_norm_g, k_norm_g,
           lambda_q1, lambda_k1, lambda_q2, lambda_k2, subln_g, w_out, norm2_g, w_router, b_router, w1, b1, w2, b2):
    bsz, seq, _ = x.shape
    inv = ROPE_THETA ** (-jnp.arange(0, DIFF_HEAD_DIM, 2, dtype=F32) / DIFF_HEAD_DIM)
    ang = positions.astype(F32).reshape(bsz * seq, 1) * inv[None, :]
    cos, sin = jnp.cos(ang), jnp.sin(ang)
    cos_t = jnp.tile(cos, (1, 2 * LANES // DIFF_HEAD_DIM))
    sin_t = jnp.tile(jnp.concatenate([-sin, sin], axis=1), (1, LANES // DIFF_HEAD_DIM))

    h = x
    for layer in range(norm1_g.shape[0]):
        lam_init = 0.8 - 0.6 * math.exp(-0.3 * layer)
        h = _layer(h, cos_t, sin_t, lam_init, norm1_g[layer], w_in[layer], conv_w[layer], conv_b[layer],
                   dt_bias[layer], a_log[layer], d_skip[layer], ssm_norm_g[layer], q_norm_g[layer],
                   k_norm_g[layer], lambda_q1[layer], lambda_k1[layer], lambda_q2[layer], lambda_k2[layer],
                   subln_g[layer], w_out[layer], norm2_g[layer], w_router[layer], b_router[layer],
                   w1[layer], b1[layer], w2[layer], b2[layer])
    return h
```

```python
import functools
import math

import jax
import jax.numpy as jnp
from jax import lax
from jax.experimental import pallas as pl
from jax.experimental.pallas import tpu as pltpu

F32 = jnp.float32
BF16 = jnp.bfloat16
U32 = jnp.uint32
I32 = jnp.int32

D_MODEL = 2048
D_SSM = 1024
D_ATTN = 1024
SSM_HEAD_DIM = 64
SSM_HEADS = 16
SSM_GROUPS = 2
SSM_STATE = 128
CONV_WIDTH = 4
CHUNK = 128
DIFF_HEAD_DIM = 64
DIFF_V_DIM = 128
DIFF_HEADS = 8
D_QK = 1024
D_V = 1024
ROPE_THETA = 10000.0
N_EXPERTS = 32
TOP_K = 4
D_EXPERT = 2048
SWIGLU_ALPHA = 1.702
SWIGLU_LIMIT = 7.0
NORM_EPS = 1e-5
LANES = 128
SUBLANES = 8
HALF = D_MODEL // 2

BC_W = SSM_GROUPS * SSM_STATE
PROJ_COLS = D_SSM + D_SSM + D_QK + D_QK + D_V + 2 * BC_W + LANES
INPROJ_TN = 1536
PROJ_PAD = -PROJ_COLS % INPROJ_TN

MOE_TM = 512
COMBINE_TC = 256
VMEM_LIMIT = 56 * 1024 * 1024


def _cparams(n_axes, vmem=VMEM_LIMIT):
    return pltpu.CompilerParams(dimension_semantics=("arbitrary",) * n_axes, vmem_limit_bytes=vmem)


def _sigmoid(v):
    return 1.0 / (1.0 + jnp.exp(-v))


def _pack2(lo, hi):
    return pltpu.pack_elementwise([lo, hi], packed_dtype=BF16)


def _unpack2(w, index):
    return pltpu.unpack_elementwise(w, index=index, packed_dtype=BF16, unpacked_dtype=F32)


def _inproj_body(x_ref, g_ref, w_ref, o_ref, u_ref):
    @pl.when(pl.program_id(1) == 0)
    def _():
        x = x_ref[...]
        ms = jnp.mean(x * x, axis=-1, keepdims=True)
        u_ref[...] = (x * lax.rsqrt(ms + NORM_EPS) * g_ref[...]).astype(BF16)

    o_ref[...] = jnp.dot(u_ref[...], w_ref[...], preferred_element_type=F32)


def _in_proj(x2, g, w_p):
    t, d = x2.shape
    ncol = w_p.shape[1]
    tm, tn = 1024, INPROJ_TN
    return pl.pallas_call(
        _inproj_body,
        grid=(t // tm, ncol // tn),
        in_specs=[pl.BlockSpec((tm, d), lambda i, j: (i, 0)),
                  pl.BlockSpec((1, d), lambda i, j: (0, 0)),
                  pl.BlockSpec((d, tn), lambda i, j: (0, j))],
        out_specs=pl.BlockSpec((tm, tn), lambda i, j: (i, j)),
        out_shape=jax.ShapeDtypeStruct((t, ncol), F32),
        scratch_shapes=[pltpu.VMEM((tm, d), BF16)],
        compiler_params=_cparams(2),
        name="in_proj",
    )(x2, g, w_p)


def _ssd_body(z_ref, x_ref, b_ref, c_ref, dt_ref, cwx_ref, cwb_ref, cwc_ref, cbx_ref, cbb_ref, cbc_ref,
              dtb_ref, alog_ref, dsk_ref, ng_ref, e_ref, y_ref, tx_ref, tb_ref, tc_ref, st_ref):
    @pl.when(pl.program_id(1) == 0)
    def _():
        tx_ref[...] = jnp.zeros_like(tx_ref)
        tb_ref[...] = jnp.zeros_like(tb_ref)
        tc_ref[...] = jnp.zeros_like(tc_ref)
        st_ref[...] = jnp.zeros_like(st_ref)

    row8 = lax.broadcasted_iota(I32, (SUBLANES, 1), 0)

    def conv_silu(u_ref, t_ref, w_ref, bias_ref):
        u = u_ref[...]
        tail = t_ref[...]
        w = w_ref[...]
        acc = u * w[CONV_WIDTH - 1:CONV_WIDTH, :] + bias_ref[...]
        for s in range(1, CONV_WIDTH):
            ru = pltpu.roll(u, s, 0)
            rt = pltpu.roll(tail, s, 0)
            head = jnp.where(row8 < s, rt, ru[0:SUBLANES, :])
            sh = jnp.concatenate([head, ru[SUBLANES:, :]], axis=0)
            acc = acc + sh * w[CONV_WIDTH - 1 - s:CONV_WIDTH - s, :]
        t_ref[...] = u[CHUNK - SUBLANES:CHUNK, :]
        return acc * _sigmoid(acc)

    xs = conv_silu(x_ref, tx_ref, cwx_ref, cbx_ref)
    bm = conv_silu(b_ref, tb_ref, cwb_ref, cbb_ref)
    cm = conv_silu(c_ref, tc_ref, cwc_ref, cbc_ref)

    dtr = dt_ref[...] + dtb_ref[...]
    dt = jnp.maximum(dtr, 0.0) + jnp.log1p(jnp.exp(-jnp.abs(dtr)))
    da = dt * (-jnp.exp(alog_ref[...]))

    rowi = lax.broadcasted_iota(I32, (CHUNK, 1), 0)
    acs = da
    k = 1
    while k < CHUNK:
        acs = acs + jnp.where(rowi >= k, pltpu.roll(acs, k, 0), 0.0)
        k *= 2
    acs_t = acs.T

    both = jnp.concatenate([dt, acs], axis=0)
    ex = None
    for _ in range(3):
        part = both.astype(BF16)
        term = jnp.dot(part, e_ref[...], preferred_element_type=F32)
        ex = term if ex is None else ex + term
        both = both - part.astype(F32)
    dt_f = ex[:CHUNK]
    a_f = ex[CHUNK:]
    a_last = a_f[CHUNK - 1:CHUNK, :]
    expa = jnp.exp(a_f)
    dstate = jnp.exp(a_last - a_f)
    cdec = jnp.exp(a_last)

    xdt = xs * dt_f
    xdt_b = xdt.astype(BF16)
    xds_b = (xdt * dstate).astype(BF16)

    coli = lax.broadcasted_iota(I32, (1, CHUNK), 1)
    tril = rowi >= coli
    lane = lax.broadcasted_iota(I32, (1, LANES), 1)
    hpg = SSM_HEADS // SSM_GROUPS
    gw = hpg * SSM_HEAD_DIM
    y_groups = []
    for g in range(SSM_GROUPS):
        bm_g = bm[:, g * SSM_STATE:(g + 1) * SSM_STATE]
        cm_g = cm[:, g * SSM_STATE:(g + 1) * SSM_STATE]
        cm_b = cm_g.astype(BF16)
        cb = lax.dot_general(cm_b, bm_g.astype(BF16), (((1,), (1,)), ((), ())),
                             preferred_element_type=F32)
        parts = []
        for pp in range(hpg // 2):
            p = g * (hpg // 2) + pp
            xp = xdt_b[:, p * LANES:(p + 1) * LANES]
            ys = []
            for hh in range(2):
                h = 2 * p + hh
                seg = acs[:, h:h + 1] - acs_t[h:h + 1, :]
                lm = jnp.where(tril, jnp.exp(seg), 0.0)
                ys.append(jnp.dot((cb * lm).astype(BF16), xp, preferred_element_type=F32))
            parts.append(jnp.where(lane < SSM_HEAD_DIM, ys[0], ys[1]))
        y_diag = jnp.concatenate(parts, axis=1)
        st_g = st_ref[:, g * gw:(g + 1) * gw]
        y_off = jnp.dot(cm_b, st_g.astype(BF16), preferred_element_type=F32) * expa[:, g * gw:(g + 1) * gw]
        new = jnp.dot(bm_g.T.astype(BF16), xds_b[:, g * gw:(g + 1) * gw], preferred_element_type=F32)
        st_ref[:, g * gw:(g + 1) * gw] = st_g * cdec[:, g * gw:(g + 1) * gw] + new
        y_groups.append(y_diag + y_off)

    zz = z_ref[...]
    gate = zz * _sigmoid(zz)
    dsk = dsk_ref[...]
    ng = ng_ref[...]
    outs = []
    for g in range(SSM_GROUPS):
        sl = slice(g * gw, (g + 1) * gw)
        yg = (y_groups[g] + xs[:, sl] * dsk[:, sl]) * gate[:, sl]
        ms = jnp.mean(yg * yg, axis=-1, keepdims=True)
        outs.append(yg * lax.rsqrt(ms + NORM_EPS) * ng[:, sl])
    y_ref[...] = jnp.concatenate(outs, axis=1).astype(BF16)


def _ssd(proj, bsz, seq, cw, cb, dt_bias, a_log, d_skip, norm_g):
    t = bsz * seq
    nc = seq // CHUNK
    cwx, cwb, cwc = cw[:, :D_SSM], cw[:, D_SSM:D_SSM + BC_W], cw[:, D_SSM + BC_W:]
    cb = cb[None, :]
    cbx, cbb, cbc = cb[:, :D_SSM], cb[:, D_SSM:D_SSM + BC_W], cb[:, D_SSM + BC_W:]
    pad = LANES - SSM_HEADS
    dtb = jnp.pad(dt_bias, (0, pad))[None, :]
    alog = jnp.pad(a_log, (0, pad))[None, :]
    dsk = jnp.repeat(d_skip, SSM_HEAD_DIM)[None, :]
    expand = (jnp.arange(D_SSM)[None, :] // SSM_HEAD_DIM == jnp.arange(LANES)[:, None]).astype(BF16)

    def row(b, c):
        return b * nc + c

    def full(shape):
        return pl.BlockSpec(shape, lambda b, c: (0, 0))

    bc0 = (2 * D_SSM + 2 * D_QK + D_V) // BC_W
    dt0 = (PROJ_COLS - LANES) // LANES
    return pl.pallas_call(
        _ssd_body,
        grid=(bsz, nc),
        in_specs=[pl.BlockSpec((CHUNK, D_SSM), lambda b, c: (row(b, c), 0)),
                  pl.BlockSpec((CHUNK, D_SSM), lambda b, c: (row(b, c), 1)),
                  pl.BlockSpec((CHUNK, BC_W), lambda b, c: (row(b, c), bc0)),
                  pl.BlockSpec((CHUNK, BC_W), lambda b, c: (row(b, c), bc0 + 1)),
                  pl.BlockSpec((CHUNK, LANES), lambda b, c: (row(b, c), dt0)),
                  full((CONV_WIDTH, D_SSM)), full((CONV_WIDTH, BC_W)), full((CONV_WIDTH, BC_W)),
                  full((1, D_SSM)), full((1, BC_W)), full((1, BC_W)),
                  full((1, LANES)), full((1, LANES)), full((1, D_SSM)), full((1, D_SSM)),
                  full((LANES, D_SSM))],
        out_specs=pl.BlockSpec((CHUNK, D_SSM), lambda b, c: (row(b, c), 0)),
        out_shape=jax.ShapeDtypeStruct((t, D_SSM), BF16),
        scratch_shapes=[pltpu.VMEM((SUBLANES, D_SSM), F32), pltpu.VMEM((SUBLANES, BC_W), F32),
                        pltpu.VMEM((SUBLANES, BC_W), F32), pltpu.VMEM((SSM_STATE, D_SSM), F32)],
        compiler_params=_cparams(2),
        name="ssd",
    )(proj, proj, proj, proj, proj, cwx, cwb, cwc, cbx, cbb, cbc, dtb, alog, dsk, norm_g[None, :], expand)


def _qkprep_body(q_ref, k_ref, v_ref, cos_ref, sin_ref, gq_ref, gk_ref, gm_ref, qo_ref, ko_ref, vo_ref):
    reps = D_QK // LANES
    cos = jnp.tile(cos_ref[...], (1, reps))
    sin = jnp.tile(sin_ref[...], (1, reps))
    lane = lax.broadcasted_iota(I32, (1, D_QK), 1)
    first = (lane & (DIFF_HEAD_DIM - 1)) < DIFF_HEAD_DIM // 2
    gm = gm_ref[...]

    def prep(t, g, scale):
        sq = t * t
        hi = sq.astype(BF16)
        lo = (sq - hi.astype(F32)).astype(BF16)
        ss = jnp.dot(hi, gm, preferred_element_type=F32) + jnp.dot(lo, gm, preferred_element_type=F32)
        tn = t * lax.rsqrt(ss * (1.0 / DIFF_HEAD_DIM) + NORM_EPS) * g
        half = DIFF_HEAD_DIM // 2
        rot = jnp.where(first, pltpu.roll(tn, D_QK - half, 1), pltpu.roll(tn, half, 1))
        return (tn * cos + rot * sin) * scale

    qo_ref[...] = prep(q_ref[...], gq_ref[...], DIFF_HEAD_DIM ** -0.5 * math.log2(math.e)).astype(BF16)
    ko_ref[...] = prep(k_ref[...], gk_ref[...], 1.0).astype(BF16)
    vo_ref[...] = v_ref[...].astype(BF16)


def _qk_prep(proj, cos_t, sin_t, gq, gk):
    t = proj.shape[0]
    tm = 512
    gsum =(jnp.arange(D_QK)[:, None] // DIFF_HEAD_DIM == jnp.arange(D_QK)[None, :] // DIFF_HEAD_DIM).astype(BF16)
    out = jax.ShapeDtypeStruct((t, D_QK), BF16)
    return pl.pallas_call(
        _qkprep_body,
        grid=(t // tm,),
        in_specs=[pl.BlockSpec((tm, D_QK), lambda i: (i, 2)),
                  pl.BlockSpec((tm, D_QK), lambda i: (i, 3)),
                  pl.BlockSpec((tm, D_V), lambda i: (i, 4)),
                  pl.BlockSpec((tm, LANES), lambda i: (i, 0)),
                  pl.BlockSpec((tm, LANES), lambda i: (i, 0)),
                  pl.BlockSpec((1, D_QK), lambda i: (0, 0)),
                  pl.BlockSpec((1, D_QK), lambda i: (0, 0)),
                  pl.BlockSpec((D_QK, D_QK), lambda i: (0, 0))],
        out_specs=[pl.BlockSpec((tm, D_QK), lambda i: (i, 0))] * 3,
        out_shape=[out, out, out],
        compiler_params=_cparams(1),
        name="qk_prep",
    )(proj, proj, proj, cos_t, sin_t, gq, gk, gsum)


KV_UNROLL = 4


HEADS_PER_STEP = 2


def _attn_body(lam_ref, q_ref, k_ref, v_ref, sg_ref, o_ref, qs_ref, vx_ref, m_ref, acc_ref, *, tq, row_chunk,
               out_scale):
    i = pl.program_id(2)
    heads = range(HEADS_PER_STEP)

    @pl.when(i == 0)
    def _():
        for hh in heads:
            vx_ref[hh, :, 0:LANES] = v_ref[:, hh * LANES:(hh + 1) * LANES]
            vx_ref[hh, :, LANES:2 * LANES] = jnp.ones((vx_ref.shape[1], LANES), BF16)

    lane = lax.broadcasted_iota(I32, (1, LANES), 1)
    for hh in heads:
        q = q_ref[:, hh * LANES:(hh + 1) * LANES]
        zero = jnp.zeros_like(q)
        qs_ref[hh, 0:tq, :] = jnp.where(lane < DIFF_HEAD_DIM, q, zero)
        qs_ref[hh, tq:2 * tq, :] = jnp.where(lane >= DIFF_HEAD_DIM, q, zero)
    m_ref[...] = jnp.full_like(m_ref, -jnp.inf)
    acc_ref[...] = jnp.zeros_like(acc_ref)

    def step(j, masked):
        start = pl.multiple_of(j * tq, tq)
        for rc in range(2 * tq // row_chunk):
            rows = slice(rc * row_chunk, (rc + 1) * row_chunk)
            for hh in heads:
                k = k_ref[pl.ds(start, tq), hh * LANES:(hh + 1) * LANES]
                s = lax.dot_general(qs_ref[hh, rows, :], k, (((1,), (1,)), ((), ())),
                                    preferred_element_type=F32)
                if masked:
                    r = lax.broadcasted_iota(I32, (row_chunk, tq), 0) + (rc * row_chunk) % tq
                    c = lax.broadcasted_iota(I32, (row_chunk, tq), 1)
                    s = jnp.where(c <= r, s, -jnp.inf)
                m_prev = m_ref[hh, rows, :]
                m_new = jnp.maximum(m_prev, jnp.max(s, axis=1, keepdims=True))
                alpha = jnp.exp2(m_prev - m_new)
                p = jnp.exp2(s - jnp.tile(m_new, (1, tq // LANES)))
                pv = jnp.dot(p.astype(BF16), vx_ref[hh, pl.ds(start, tq), :], preferred_element_type=F32)
                acc_ref[hh, rows, :] = jnp.tile(alpha, (1, 2)) * acc_ref[hh, rows, :] + pv
                m_ref[hh, rows, :] = m_new

    def off_diagonal_group(jj, carry):
        for u in range(KV_UNROLL):
            step(KV_UNROLL * jj + u, False)
        return carry

    lax.fori_loop(0, i // KV_UNROLL, off_diagonal_group, 0)
    done = (i // KV_UNROLL) * KV_UNROLL
    for rem in range(KV_UNROLL):
        @pl.when(i - done == rem)
        def _(rem=rem):
            for u in range(rem):
                step(done + u, False)
            step(i, True)

    for hh in heads:
        acc = acc_ref[hh]
        o = acc[:, 0:LANES] / acc[:, LANES:2 * LANES]
        d = o[0:tq, :] - lam_ref[0] * o[tq:2 * tq, :]
        ms = jnp.mean(d * d, axis=-1, keepdims=True)
        o_ref[:, hh * LANES:(hh + 1) * LANES] = (d * lax.rsqrt(ms + NORM_EPS) * sg_ref[...]
                                                 * out_scale).astype(BF16)


def _attention(qr, kr, vb, lam, subln_g, bsz, seq, out_scale):
    t = bsz * seq
    tq = 512
    nq = seq // tq
    hps = HEADS_PER_STEP
    gw = hps * LANES
    body = functools.partial(_attn_body, tq=tq, row_chunk=128, out_scale=out_scale)
    grid_spec = pltpu.PrefetchScalarGridSpec(
        num_scalar_prefetch=1,
        grid=(bsz, DIFF_HEADS // hps, nq),
        in_specs=[pl.BlockSpec((tq, gw), lambda b, h, i, lam: (b * nq + i, h)),
                  pl.BlockSpec((seq, gw), lambda b, h, i, lam: (b, h)),
                  pl.BlockSpec((seq, gw), lambda b, h, i, lam: (b, h)),
                  pl.BlockSpec((1, LANES), lambda b, h, i, lam: (0, 0))],
        out_specs=pl.BlockSpec((tq, gw), lambda b, h, i, lam: (b * nq + i, h)),
        scratch_shapes=[pltpu.VMEM((hps, 2 * tq, LANES), BF16), pltpu.VMEM((hps, seq, 2 * LANES), BF16),
                        pltpu.VMEM((hps, 2 * tq, LANES), F32), pltpu.VMEM((hps, 2 * tq, 2 * LANES), F32)],
    )
    return pl.pallas_call(
        body, grid_spec=grid_spec,
        out_shape=jax.ShapeDtypeStruct((t, D_ATTN), BF16),
        compiler_params=_cparams(3),
        name="diff_attention",
    )(lam, qr, kr, vb, subln_g[None, :])


def _outproj_body(x_ref, ys_ref, ya_ref, wt_ref, wb_ref, g2_ref, wrh_ref, wrl_ref, br_ref, ltri_ref,
                  h1_ref, hp_ref, ti_ref, tg_ref, cnt_out_ref, cnt_ref):
    h1 = (x_ref[...] + jnp.dot(ys_ref[...], wt_ref[...], preferred_element_type=F32)
          + jnp.dot(ya_ref[...], wb_ref[...], preferred_element_type=F32))
    h1_ref[...] = h1
    ms = jnp.mean(h1 * h1, axis=-1, keepdims=True)
    h2 = h1 * lax.rsqrt(ms + NORM_EPS) * g2_ref[...]
    hp_ref[...] = _pack2(h2[:, :HALF], h2[:, HALF:])

    h2_hi = h2.astype(BF16)
    h2_lo = (h2 - h2_hi.astype(F32)).astype(BF16)
    logits = (jnp.dot(h2_hi, wrh_ref[...], preferred_element_type=F32)
              + jnp.dot(h2_hi, wrl_ref[...], preferred_element_type=F32)
              + jnp.dot(h2_lo, wrh_ref[...], preferred_element_type=F32) + br_ref[...])
    lane = lax.broadcasted_iota(I32, (1, LANES), 1)
    lanef = lane.astype(F32)
    cur = jnp.where(lane < N_EXPERTS, logits, -jnp.inf)
    vals, idxs = [], []
    for _ in range(TOP_K):
        m = jnp.max(cur, axis=1, keepdims=True)
        idx = jnp.min(jnp.where(cur == m, lanef, float(LANES)), axis=1, keepdims=True)
        vals.append(m)
        idxs.append(idx)
        cur = jnp.where(lanef == idx, -jnp.inf, cur)
    es = [jnp.exp(v - vals[0]) for v in vals]
    den = es[0] + es[1] + es[2] + es[3]

    @pl.when(pl.program_id(0) == 0)
    def _():
        cnt_ref[...] = jnp.zeros_like(cnt_ref)

    base = cnt_ref[...]
    ltri = ltri_ref[...]
    ti = jnp.zeros(logits.shape, F32)
    tg = jnp.zeros(logits.shape, F32)
    for kk in range(TOP_K):
        oh = lanef == idxs[kk]
        ohf = jnp.where(oh, 1.0, 0.0)
        pre = jnp.dot(ltri, ohf.astype(BF16), preferred_element_type=F32)
        rank = jnp.sum(jnp.where(oh, pre + base, 0.0), axis=1, keepdims=True)
        base = base + jnp.sum(ohf, axis=0, keepdims=True)
        ti = jnp.where(lane == kk, idxs[kk], ti)
        ti = jnp.where(lane == TOP_K + kk, rank, ti)
        tg = jnp.where(lane == kk, es[kk] / den, tg)
    cnt_ref[...] = base
    ti_ref[...] = ti.astype(I32)
    tg_ref[...] = tg
    cnt_out_ref[...] = jnp.broadcast_to(base, cnt_out_ref.shape).astype(I32)


def _out_proj_router(x2, y_ssm, y_attn, w_top, w_bot, g2, w_r, b_r):
    t, d = x2.shape
    tm = 512
    w_rh = w_r.astype(BF16)
    w_rl = (w_r - w_rh.astype(F32)).astype(BF16)
    ltri = (jnp.arange(tm)[:, None] > jnp.arange(tm)[None, :]).astype(BF16)

    def full(shape):
        return pl.BlockSpec(shape, lambda i: (0, 0))

    return pl.pallas_call(
        _outproj_body,
        grid=(t // tm,),
        in_specs=[pl.BlockSpec((tm, d), lambda i: (i, 0)),
                  pl.BlockSpec((tm, D_SSM), lambda i: (i, 0)),
                  pl.BlockSpec((tm, D_ATTN), lambda i: (i, 0)),
                  full((D_SSM, d)), full((D_ATTN, d)), full((1, d)), full((d, LANES)), full((d, LANES)),
                  full((1, LANES)), full((tm, tm))],
        out_specs=[pl.BlockSpec((tm, d), lambda i: (i, 0)),
                   pl.BlockSpec((tm, HALF), lambda i: (i, 0)),
                   pl.BlockSpec((tm, LANES), lambda i: (i, 0)),
                   pl.BlockSpec((tm, LANES), lambda i: (i, 0)),
                   full((SUBLANES, LANES))],
        out_shape=[jax.ShapeDtypeStruct((t, d), F32), jax.ShapeDtypeStruct((t, HALF), U32),
                   jax.ShapeDtypeStruct((t, LANES), I32), jax.ShapeDtypeStruct((t, LANES), F32),
                   jax.ShapeDtypeStruct((SUBLANES, LANES), I32)],
        scratch_shapes=[pltpu.VMEM((1, LANES), F32)],
        compiler_params=_cparams(1),
        name="out_proj_router",
    )(x2, y_ssm, y_attn, w_top, w_bot, g2, w_rh, w_rl, b_r, ltri)


def _row_copy(src_hbm, src_row, dst_ref, dst_row, sem):
    return pltpu.make_async_copy(src_hbm.at[pl.ds(src_row, 1), :], dst_ref.at[pl.ds(dst_row, 1), :], sem)


DMA_UNROLL = 8


def _zero_rows_body(o_ref):
    zero = jnp.zeros(o_ref.shape, F32)
    o_ref[...] = _pack2(zero, zero)


def _zero_rows(n_rows, rows_per_step):
    return pl.pallas_call(
        _zero_rows_body,
        grid=(n_rows // rows_per_step,),
        out_specs=pl.BlockSpec((rows_per_step, HALF), lambda i: (i, 0)),
        out_shape=jax.ShapeDtypeStruct((n_rows, HALF), U32),
        compiler_params=_cparams(1),
        name="zero_rows",
    )()


def _dispatch_body(pos_ref, x_ref, init_hbm, dst_hbm, sem, *, tokens):
    del init_hbm
    n = TOP_K * tokens

    def start(g, carry):
        for u in range(DMA_UNROLL):
            a = g * DMA_UNROLL + u
            t = g * (DMA_UNROLL // TOP_K) + u // TOP_K
            _row_copy(x_ref, t, dst_hbm, pos_ref[0, 0, a], sem).start(priority=u % 2)
        return carry

    def wait(g, carry):
        for u in range(DMA_UNROLL):
            t = g * (DMA_UNROLL // TOP_K) + u // TOP_K
            _row_copy(x_ref, t, dst_hbm, 0, sem).wait()
        return carry

    lax.fori_loop(0, n // DMA_UNROLL, start, 0)
    lax.fori_loop(0, n // DMA_UNROLL, wait, 0)


def _dispatch_rows(src, pos, n_rows):
    t, w = src.shape
    tokens = 512
    nsteps = t // tokens
    return pl.pallas_call(
        functools.partial(_dispatch_body, tokens=tokens),
        grid=(nsteps,),
        in_specs=[pl.BlockSpec((1, 1, TOP_K * tokens), lambda i: (i, 0, 0), memory_space=pltpu.SMEM),
                  pl.BlockSpec((tokens, w), lambda i: (i, 0)),
                  pl.BlockSpec(memory_space=pl.ANY)],
        out_specs=pl.BlockSpec(memory_space=pl.ANY),
        out_shape=jax.ShapeDtypeStruct((n_rows, w), src.dtype),
        scratch_shapes=[pltpu.SemaphoreType.DMA(())],
        input_output_aliases={2: 0},
        compiler_params=_cparams(1),
        name="dispatch_rows",
    )(pos.reshape(nsteps, 1, TOP_K * tokens), src, _zero_rows(n_rows, 2048))


MXU_W = 256


def _moe_up_body(te_ref, nv_ref, x_ref, w_ref, p_ref, bg_ref, bl_ref, o_ref, wg_ref, wl_ref):
    i = pl.program_id(1)
    new_expert = jnp.logical_or(i == 0, te_ref[i] != te_ref[jnp.maximum(i - 1, 0)])

    @pl.when(new_expert)
    def _():
        half = MXU_W // 2
        for c in range(w_ref.shape[1] // MXU_W):
            wc = w_ref[:, c * MXU_W:(c + 1) * MXU_W].astype(BF16)
            r = jnp.dot(wc, p_ref[...], preferred_element_type=F32)
            wg_ref[:, c * half:(c + 1) * half] = r[:, :half].astype(BF16)
            wl_ref[:, c * half:(c + 1) * half] = r[:, half:].astype(BF16)

    @pl.when(i < nv_ref[0])
    def _():
        w = x_ref[...]
        lo = _unpack2(w, 0).astype(BF16)
        hi = _unpack2(w, 1).astype(BF16)

        def up(ws_ref, b_ref):
            return (jnp.dot(lo, ws_ref[0:HALF, :], preferred_element_type=F32)
                    + jnp.dot(hi, ws_ref[HALF:, :], preferred_element_type=F32) + b_ref[...])

        glu = jnp.minimum(up(wg_ref, bg_ref), SWIGLU_LIMIT)
        lin = jnp.clip(up(wl_ref, bl_ref), -SWIGLU_LIMIT, SWIGLU_LIMIT)
        o_ref[...] = (glu * _sigmoid(SWIGLU_ALPHA * glu) * (lin + 1.0)).astype(BF16)

    @pl.when(i >= nv_ref[0])
    def _():
        o_ref[...] = jnp.zeros_like(o_ref)


def _moe_up(tile_expert, n_valid, xs, w1, b1g, b1l):
    r = xs.shape[0]
    tm, tn = MOE_TM, 1024
    nt = r // tm
    j = jnp.arange(MXU_W)
    perm = (j[None, :] == jnp.where(j % 2 == 0, j // 2, MXU_W // 2 + j // 2)[:, None]).astype(BF16)
    grid_spec = pltpu.PrefetchScalarGridSpec(
        num_scalar_prefetch=2,
        grid=(D_EXPERT // tn, nt),
        in_specs=[pl.BlockSpec((tm, HALF), lambda n, i, te, nv: (i, 0)),
                  pl.BlockSpec((None, D_MODEL, 2 * tn), lambda n, i, te, nv: (te[i], 0, n)),
                  pl.BlockSpec((MXU_W, MXU_W), lambda n, i, te, nv: (0, 0)),
                  pl.BlockSpec((None, 1, tn), lambda n, i, te, nv: (te[i], 0, n)),
                  pl.BlockSpec((None, 1, tn), lambda n, i, te, nv: (te[i], 0, n))],
        out_specs=pl.BlockSpec((tm, tn), lambda n, i, te, nv: (i, n)),
        scratch_shapes=[pltpu.VMEM((D_MODEL, tn), BF16), pltpu.VMEM((D_MODEL, tn), BF16)],
    )
    return pl.pallas_call(
        _moe_up_body, grid_spec=grid_spec,
        out_shape=jax.ShapeDtypeStruct((r, D_EXPERT), BF16),
        compiler_params=_cparams(2),
        name="moe_up",
    )(tile_expert, n_valid, xs, w1, perm, b1g, b1l)


def _moe_down_body(te_ref, nv_ref, a_ref, w_ref, b_ref, o_ref, wb_ref):
    i = pl.program_id(0)
    new_expert = jnp.logical_or(i == 0, te_ref[i] != te_ref[jnp.maximum(i - 1, 0)])

    @pl.when(new_expert)
    def _():
        wb_ref[...] = w_ref[...].astype(BF16)

    @pl.when(i < nv_ref[0])
    def _():
        y = jnp.dot(a_ref[...], wb_ref[...], preferred_element_type=F32) + b_ref[...]
        o_ref[...] = _pack2(y[:, :HALF], y[:, HALF:])

    @pl.when(i >= nv_ref[0])
    def _():
        zero = jnp.zeros(o_ref.shape, F32)
        o_ref[...] = _pack2(zero, zero)


def _moe_down(tile_expert, n_valid, act, w2b, b2):
    r = act.shape[0]
    tm = MOE_TM
    nt = r // tm
    grid_spec = pltpu.PrefetchScalarGridSpec(
        num_scalar_prefetch=2,
        grid=(nt,),
        in_specs=[pl.BlockSpec((tm, D_EXPERT), lambda i, te, nv: (i, 0)),
                  pl.BlockSpec((None, D_EXPERT, D_MODEL), lambda i, te, nv: (te[i], 0, 0)),
                  pl.BlockSpec((None, 1, D_MODEL), lambda i, te, nv: (te[i], 0, 0))],
        out_specs=pl.BlockSpec((tm, HALF), lambda i, te, nv: (i, 0)),
        scratch_shapes=[pltpu.VMEM((D_EXPERT, D_MODEL), BF16)],
    )
    return pl.pallas_call(
        _moe_down_body, grid_spec=grid_spec,
        out_shape=jax.ShapeDtypeStruct((r, HALF), U32),
        compiler_params=_cparams(1),
        name="moe_down",
    )(tile_expert, n_valid, act, w2b, b2)


def _combine_body(idx_ref, idx_next_ref, y_hbm, h1_ref, g_ref, o_ref, buf_ref, sem, *, tc):
    n = TOP_K * tc
    i = pl.program_id(0)
    slot = i % 2

    def issue(ids_ref, s):
        def start(g, carry):
            for u in range(DMA_UNROLL):
                a = g * DMA_UNROLL + u
                _row_copy(y_hbm, ids_ref[0, 0, a], buf_ref.at[s], a, sem.at[s]).start(priority=u % 2)
            return carry
        lax.fori_loop(0, n // DMA_UNROLL, start, 0)

    @pl.when(i == 0)
    def _():
        issue(idx_ref, 0)

    @pl.when(i + 1 < pl.num_programs(0))
    def _():
        issue(idx_next_ref, 1 - slot)

    def wait(g, carry):
        for u in range(DMA_UNROLL):
            a = g * DMA_UNROLL + u
            _row_copy(y_hbm, 0, buf_ref.at[slot], a, sem.at[slot]).wait()
        return carry

    lax.fori_loop(0, n // DMA_UNROLL, wait, 0)

    g = g_ref[...]
    lo = h1_ref[:, :HALF]
    hi = h1_ref[:, HALF:]
    for kk in range(TOP_K):
        w = buf_ref[slot, kk * tc:(kk + 1) * tc, :]
        gk = g[:, kk:kk + 1]
        lo = lo + gk * _unpack2(w, 0)
        hi = hi + gk * _unpack2(w, 1)
    o_ref[:, :HALF] = lo
    o_ref[:, HALF:] = hi


def _combine(pos_km, y_rows, h1, gates):
    t, d = h1.shape
    tc = COMBINE_TC
    nsteps = t // tc
    return pl.pallas_call(
        functools.partial(_combine_body, tc=tc),
        grid=(nsteps,),
        in_specs=[pl.BlockSpec((1, 1, TOP_K * tc), lambda i: (i, 0, 0), memory_space=pltpu.SMEM),
                  pl.BlockSpec((1, 1, TOP_K * tc), lambda i: (jnp.minimum(i + 1, nsteps - 1), 0, 0),
                               memory_space=pltpu.SMEM),
                  pl.BlockSpec(memory_space=pl.ANY),
                  pl.BlockSpec((tc, d), lambda i: (i, 0)),
                  pl.BlockSpec((tc, LANES), lambda i: (i, 0))],
        out_specs=pl.BlockSpec((tc, d), lambda i: (i, 0)),
        out_shape=jax.ShapeDtypeStruct((t, d), F32),
        scratch_shapes=[pltpu.VMEM((2, TOP_K * tc, HALF), U32), pltpu.SemaphoreType.DMA((2,))],
        compiler_params=_cparams(1),
        name="moe_combine",
    )(pos_km, pos_km, y_rows, h1, gates)


def _route(ti, counts, tm):
    t = ti.shape[0]
    n_assign = t * TOP_K
    n_rows = n_assign + N_EXPERTS * tm
    nt = n_rows // tm
    top_e = ti[:, :TOP_K]
    rank = ti[:, TOP_K:2 * TOP_K]
    counts = counts[0, :N_EXPERTS]
    tiles_per = (counts + (tm - 1)) // tm
    tile_end = jnp.cumsum(tiles_per)
    pstart = (tile_end - tiles_per) * tm
    experts = jnp.arange(N_EXPERTS, dtype=I32)
    onehot = top_e[:, :, None] == experts[None, None, :]
    pos = jnp.sum(jnp.where(onehot, pstart[None, None, :], 0), axis=-1) + rank
    n_valid = tile_end[N_EXPERTS - 1]
    tile_ids = jnp.minimum(jnp.arange(nt, dtype=I32), n_valid - 1)
    tile_expert = jnp.sum((tile_end[None, :] <= tile_ids[:, None]).astype(I32), axis=1)
    tile_expert = jnp.minimum(tile_expert, N_EXPERTS - 1)
    return pos, tile_expert, n_valid.reshape(1), n_rows


def _layer(h, cos_t, sin_t, lam_init, norm1_g, w_in, conv_w, conv_b, dt_bias, a_log, d_skip, ssm_norm_g,
           q_norm_g, k_norm_g, lq1, lk1, lq2, lk2, subln_g, w_out, norm2_g, w_router, b_router, w1, b1, w2, b2):
    bsz, seq, d = h.shape
    t = bsz * seq
    x2 = h.reshape(t, d)

    o_z = 0
    o_xbc = D_SSM
    o_dt = o_xbc + D_SSM + 2 * BC_W
    o_q = o_dt + SSM_HEADS
    o_k = o_q + D_QK
    o_v = o_k + D_QK
    w_p = jnp.concatenate([
        w_in[:, o_z:o_z + D_SSM], w_in[:, o_xbc:o_xbc + D_SSM],
        w_in[:, o_q:o_q + D_QK], w_in[:, o_k:o_k + D_QK], w_in[:, o_v:o_v + D_V],
        w_in[:, o_xbc + D_SSM:o_xbc + D_SSM + 2 * BC_W],
        jnp.pad(w_in[:, o_dt:o_dt + SSM_HEADS], ((0, 0), (0, LANES - SSM_HEADS + PROJ_PAD)))],
        axis=1).astype(BF16)

    proj = _in_proj(x2, norm1_g[None, :], w_p)
    y_ssm = _ssd(proj, bsz, seq, conv_w, conv_b, dt_bias, a_log, d_skip, ssm_norm_g)

    reps = D_QK // DIFF_HEAD_DIM
    qr, kr, vb = _qk_prep(proj, cos_t, sin_t, jnp.tile(q_norm_g, reps)[None, :], jnp.tile(k_norm_g, reps)[None, :])
    lam = (jnp.exp(jnp.sum(lq1 * lk1)) - jnp.exp(jnp.sum(lq2 * lk2)) + lam_init).reshape(1).astype(F32)
    y_attn = _attention(qr, kr, vb, lam, subln_g, bsz, seq, 1.0 - lam_init)

    w_ob = w_out.astype(BF16)
    w_r = jnp.pad(w_router, ((0, 0), (0, LANES - N_EXPERTS)))
    b_r = jnp.pad(b_router, (0, LANES - N_EXPERTS))[None, :]
    h1, h2p, ti, tg, counts = _out_proj_router(x2, y_ssm, y_attn, w_ob[:D_SSM], w_ob[D_SSM:], norm2_g[None, :],
                                               w_r, b_r)

    pos, tile_expert, n_valid, n_rows = _route(ti, counts, MOE_TM)
    xs = _dispatch_rows(h2p, pos, n_rows)
    act = _moe_up(tile_expert, n_valid, xs, w1, b1[:, None, 0::2], b1[:, None, 1::2])
    y_rows = _moe_down(tile_expert, n_valid, act, w2, b2[:, None, :])

    tc = COMBINE_TC
    pos_km = pos.reshape(t // tc, tc, TOP_K).transpose(0, 2, 1).reshape(t // tc, 1, TOP_K * tc)
    out = _combine(pos_km, y_rows, h1, tg)
    return out.reshape(bsz, seq, d)


def kernel(x, positions, norm1_g, w_in, conv_w, conv_b, dt_bias, a_log, d_skip, ssm_norm_g, q_norm_g, k_norm_g,
           lambda_q1, lambda_k1, lambda_q2, lambda_k2, subln_g, w_out, norm2_g, w_router, b_router, w1, b1, w2, b2):
    bsz, seq, _ = x.shape
    inv = ROPE_THETA ** (-jnp.arange(0, DIFF_HEAD_DIM, 2, dtype=F32) / DIFF_HEAD_DIM)
    ang = positions.astype(F32).reshape(bsz * seq, 1) * inv[None, :]
    cos, sin = jnp.cos(ang), jnp.sin(ang)
    cos_t = jnp.tile(cos, (1, 2 * LANES // DIFF_HEAD_DIM))
    sin_t = jnp.tile(jnp.concatenate([-sin, sin], axis=1), (1, LANES // DIFF_HEAD_DIM))

    h = x
    for layer in range(norm1_g.shape[0]):
        lam_init = 0.8 - 0.6 * math.exp(-0.3 * layer)
        h = _layer(h, cos_t, sin_t, lam_init, norm1_g[layer], w_in[layer], conv_w[layer], conv_b[layer],
                   dt_bias[layer], a_log[layer], d_skip[layer], ssm_norm_g[layer], q_norm_g[layer],
                   k_norm_g[layer], lambda_q1[layer], lambda_k1[layer], lambda_q2[layer], lambda_k2[layer],
                   subln_g[layer], w_out[layer], norm2_g[layer], w_router[layer], b_router[layer],
                   w1[layer], b1[layer], w2[layer], b2[layer])
    return h
```

```python
import functools
import math

import jax
import jax.numpy as jnp
from jax import lax
from jax.experimental import pallas as pl
from jax.experimental.pallas import tpu as pltpu

F32 = jnp.float32
BF16 = jnp.bfloat16
U32 = jnp.uint32
I32 = jnp.int32

D_MODEL = 2048
D_SSM = 1024
D_ATTN = 1024
SSM_HEAD_DIM = 64
SSM_HEADS = 16
SSM_GROUPS = 2
SSM_STATE = 128
CONV_WIDTH = 4
CHUNK = 128
DIFF_HEAD_DIM = 64
DIFF_V_DIM = 128
DIFF_HEADS = 8
D_QK = 1024
D_V = 1024
ROPE_THETA = 10000.0
N_EXPERTS = 32
TOP_K = 4
D_EXPERT = 2048
SWIGLU_ALPHA = 1.702
SWIGLU_LIMIT = 7.0
NORM_EPS = 1e-5
LANES = 128
SUBLANES = 8
HALF = D_MODEL // 2

BC_W = SSM_GROUPS * SSM_STATE
PROJ_COLS = D_SSM + D_SSM + D_QK + D_QK + D_V + 2 * BC_W + LANES
INPROJ_TN = 1536
PROJ_PAD = -PROJ_COLS % INPROJ_TN

MOE_TM = 512
COMBINE_TC = 256
VMEM_LIMIT = 56 * 1024 * 1024


def _cparams(n_axes, vmem=VMEM_LIMIT):
    return pltpu.CompilerParams(dimension_semantics=("arbitrary",) * n_axes, vmem_limit_bytes=vmem)


def _sigmoid(v):
    return 1.0 / (1.0 + jnp.exp(-v))


def _pack2(lo, hi):
    return pltpu.pack_elementwise([lo, hi], packed_dtype=BF16)


def _unpack2(w, index):
    return pltpu.unpack_elementwise(w, index=index, packed_dtype=BF16, unpacked_dtype=F32)


def _inproj_body(x_ref, g_ref, w_ref, o_ref, u_ref):
    @pl.when(pl.program_id(1) == 0)
    def _():
        x = x_ref[...]
        ms = jnp.mean(x * x, axis=-1, keepdims=True)
        u_ref[...] = (x * lax.rsqrt(ms + NORM_EPS) * g_ref[...]).astype(BF16)

    o_ref[...] = jnp.dot(u_ref[...], w_ref[...], preferred_element_type=F32)


def _in_proj(x2, g, w_p):
    t, d = x2.shape
    ncol = w_p.shape[1]
    tm, tn = 1024, INPROJ_TN
    return pl.pallas_call(
        _inproj_body,
        grid=(t // tm, ncol // tn),
        in_specs=[pl.BlockSpec((tm, d), lambda i, j: (i, 0)),
                  pl.BlockSpec((1, d), lambda i, j: (0, 0)),
                  pl.BlockSpec((d, tn), lambda i, j: (0, j))],
        out_specs=pl.BlockSpec((tm, tn), lambda i, j: (i, j)),
        out_shape=jax.ShapeDtypeStruct((t, ncol), F32),
        scratch_shapes=[pltpu.VMEM((tm, d), BF16)],
        compiler_params=_cparams(2),
        name="in_proj",
    )(x2, g, w_p)


def _ssd_body(z_ref, x_ref, b_ref, c_ref, dt_ref, cwx_ref, cwb_ref, cwc_ref, cbx_ref, cbb_ref, cbc_ref,
              dtb_ref, alog_ref, dsk_ref, ng_ref, e_ref, y_ref, tx_ref, tb_ref, tc_ref, st_ref):
    @pl.when(pl.program_id(1) == 0)
    def _():
        tx_ref[...] = jnp.zeros_like(tx_ref)
        tb_ref[...] = jnp.zeros_like(tb_ref)
        tc_ref[...] = jnp.zeros_like(tc_ref)
        st_ref[...] = jnp.zeros_like(st_ref)

    row8 = lax.broadcasted_iota(I32, (SUBLANES, 1), 0)

    def conv_silu(u_ref, t_ref, w_ref, bias_ref):
        u = u_ref[...]
        tail = t_ref[...]
        w = w_ref[...]
        acc = u * w[CONV_WIDTH - 1:CONV_WIDTH, :] + bias_ref[...]
        for s in range(1, CONV_WIDTH):
            ru = pltpu.roll(u, s, 0)
            rt = pltpu.roll(tail, s, 0)
            head = jnp.where(row8 < s, rt, ru[0:SUBLANES, :])
            sh = jnp.concatenate([head, ru[SUBLANES:, :]], axis=0)
            acc = acc + sh * w[CONV_WIDTH - 1 - s:CONV_WIDTH - s, :]
        t_ref[...] = u[CHUNK - SUBLANES:CHUNK, :]
        return acc * _sigmoid(acc)

    xs = conv_silu(x_ref, tx_ref, cwx_ref, cbx_ref)
    bm = conv_silu(b_ref, tb_ref, cwb_ref, cbb_ref)
    cm = conv_silu(c_ref, tc_ref, cwc_ref, cbc_ref)

    dtr = dt_ref[...] + dtb_ref[...]
    dt = jnp.maximum(dtr, 0.0) + jnp.log1p(jnp.exp(-jnp.abs(dtr)))
    da = dt * (-jnp.exp(alog_ref[...]))

    rowi = lax.broadcasted_iota(I32, (CHUNK, 1), 0)
    acs = da
    k = 1
    while k < CHUNK:
        acs = acs + jnp.where(rowi >= k, pltpu.roll(acs, k, 0), 0.0)
        k *= 2
    acs_t = acs.T

    both = jnp.concatenate([dt, acs], axis=0)
    ex = None
    for _ in range(3):
        part = both.astype(BF16)
        term = jnp.dot(part, e_ref[...], preferred_element_type=F32)
        ex = term if ex is None else ex + term
        both = both - part.astype(F32)
    dt_f = ex[:CHUNK]
    a_f = ex[CHUNK:]
    a_last = a_f[CHUNK - 1:CHUNK, :]
    expa = jnp.exp(a_f)
    dstate = jnp.exp(a_last - a_f)
    cdec = jnp.exp(a_last)

    xdt = xs * dt_f
    xdt_b = xdt.astype(BF16)
    xds_b = (xdt * dstate).astype(BF16)

    coli = lax.broadcasted_iota(I32, (1, CHUNK), 1)
    tril = rowi >= coli
    lane = lax.broadcasted_iota(I32, (1, LANES), 1)
    hpg = SSM_HEADS // SSM_GROUPS
    gw = hpg * SSM_HEAD_DIM
    y_groups = []
    for g in range(SSM_GROUPS):
        bm_g = bm[:, g * SSM_STATE:(g + 1) * SSM_STATE]
        cm_g = cm[:, g * SSM_STATE:(g + 1) * SSM_STATE]
        cm_b = cm_g.astype(BF16)
        cb = lax.dot_general(cm_b, bm_g.astype(BF16), (((1,), (1,)), ((), ())),
                             preferred_element_type=F32)
        parts = []
        for pp in range(hpg // 2):
            p = g * (hpg // 2) + pp
            xp = xdt_b[:, p * LANES:(p + 1) * LANES]
            ys = []
            for hh in range(2):
                h = 2 * p + hh
                seg = acs[:, h:h + 1] - acs_t[h:h + 1, :]
                lm = jnp.where(tril, jnp.exp(seg), 0.0)
                ys.append(jnp.dot((cb * lm).astype(BF16), xp, preferred_element_type=F32))
            parts.append(jnp.where(lane < SSM_HEAD_DIM, ys[0], ys[1]))
        y_diag = jnp.concatenate(parts, axis=1)
        st_g = st_ref[:, g * gw:(g + 1) * gw]
        y_off = jnp.dot(cm_b, st_g.astype(BF16), preferred_element_type=F32) * expa[:, g * gw:(g + 1) * gw]
        new = jnp.dot(bm_g.T.astype(BF16), xds_b[:, g * gw:(g + 1) * gw], preferred_element_type=F32)
        st_ref[:, g * gw:(g + 1) * gw] = st_g * cdec[:, g * gw:(g + 1) * gw] + new
        y_groups.append(y_diag + y_off)

    zz = z_ref[...]
    gate = zz * _sigmoid(zz)
    dsk = dsk_ref[...]
    ng = ng_ref[...]
    outs = []
    for g in range(SSM_GROUPS):
        sl = slice(g * gw, (g + 1) * gw)
        yg = (y_groups[g] + xs[:, sl] * dsk[:, sl]) * gate[:, sl]
        ms = jnp.mean(yg * yg, axis=-1, keepdims=True)
        outs.append(yg * lax.rsqrt(ms + NORM_EPS) * ng[:, sl])
    y_ref[...] = jnp.concatenate(outs, axis=1).astype(BF16)


def _ssd(proj, bsz, seq, cw, cb, dt_bias, a_log, d_skip, norm_g):
    t = bsz * seq
    nc = seq // CHUNK
    cwx, cwb, cwc = cw[:, :D_SSM], cw[:, D_SSM:D_SSM + BC_W], cw[:, D_SSM + BC_W:]
    cb = cb[None, :]
    cbx, cbb, cbc = cb[:, :D_SSM], cb[:, D_SSM:D_SSM + BC_W], cb[:, D_SSM + BC_W:]
    pad = LANES - SSM_HEADS
    dtb = jnp.pad(dt_bias, (0, pad))[None, :]
    alog = jnp.pad(a_log, (0, pad))[None, :]
    dsk = jnp.repeat(d_skip, SSM_HEAD_DIM)[None, :]
    expand = (jnp.arange(D_SSM)[None, :] // SSM_HEAD_DIM == jnp.arange(LANES)[:, None]).astype(BF16)

    def row(b, c):
        return b * nc + c

    def full(shape):
        return pl.BlockSpec(shape, lambda b, c: (0, 0))

    bc0 = (2 * D_SSM + 2 * D_QK + D_V) // BC_W
    dt0 = (PROJ_COLS - LANES) // LANES
    return pl.pallas_call(
        _ssd_body,
        grid=(bsz, nc),
        in_specs=[pl.BlockSpec((CHUNK, D_SSM), lambda b, c: (row(b, c), 0)),
                  pl.BlockSpec((CHUNK, D_SSM), lambda b, c: (row(b, c), 1)),
                  pl.BlockSpec((CHUNK, BC_W), lambda b, c: (row(b, c), bc0)),
                  pl.BlockSpec((CHUNK, BC_W), lambda b, c: (row(b, c), bc0 + 1)),
                  pl.BlockSpec((CHUNK, LANES), lambda b, c: (row(b, c), dt0)),
                  full((CONV_WIDTH, D_SSM)), full((CONV_WIDTH, BC_W)), full((CONV_WIDTH, BC_W)),
                  full((1, D_SSM)), full((1, BC_W)), full((1, BC_W)),
                  full((1, LANES)), full((1, LANES)), full((1, D_SSM)), full((1, D_SSM)),
                  full((LANES, D_SSM))],
        out_specs=pl.BlockSpec((CHUNK, D_SSM), lambda b, c: (row(b, c), 0)),
        out_shape=jax.ShapeDtypeStruct((t, D_SSM), BF16),
        scratch_shapes=[pltpu.VMEM((SUBLANES, D_SSM), F32), pltpu.VMEM((SUBLANES, BC_W), F32),
                        pltpu.VMEM((SUBLANES, BC_W), F32), pltpu.VMEM((SSM_STATE, D_SSM), F32)],
        compiler_params=_cparams(2),
        name="ssd",
    )(proj, proj, proj, proj, proj, cwx, cwb, cwc, cbx, cbb, cbc, dtb, alog, dsk, norm_g[None, :], expand)


def _qkprep_body(q_ref, k_ref, v_ref, cos_ref, sin_ref, gq_ref, gk_ref, gm_ref, qo_ref, ko_ref, vo_ref):
    reps = D_QK // LANES
    cos = jnp.tile(cos_ref[...], (1, reps))
    sin = jnp.tile(sin_ref[...], (1, reps))
    lane = lax.broadcasted_iota(I32, (1, D_QK), 1)
    first = (lane & (DIFF_HEAD_DIM - 1)) < DIFF_HEAD_DIM // 2
    gm = gm_ref[...]

    def prep(t, g, scale):
        sq = t * t
        hi = sq.astype(BF16)
        lo = (sq - hi.astype(F32)).astype(BF16)
        ss = jnp.concatenate(
            [jnp.dot(hi[:, c:c + MXU_W], gm, preferred_element_type=F32)
             + jnp.dot(lo[:, c:c + MXU_W], gm, preferred_element_type=F32) for c in range(0, D_QK, MXU_W)], axis=1)
        tn = t * lax.rsqrt(ss * (1.0 / DIFF_HEAD_DIM) + NORM_EPS) * g
        half = DIFF_HEAD_DIM // 2
        rot = jnp.where(first, pltpu.roll(tn, D_QK - half, 1), pltpu.roll(tn, half, 1))
        return (tn * cos + rot * sin) * scale

    qo_ref[...] = prep(q_ref[...], gq_ref[...], DIFF_HEAD_DIM ** -0.5 * math.log2(math.e)).astype(BF16)
    ko_ref[...] = prep(k_ref[...], gk_ref[...], 1.0).astype(BF16)
    vo_ref[...] = v_ref[...].astype(BF16)


def _qk_prep(proj, cos_t, sin_t, gq, gk):
    t = proj.shape[0]
    tm = 512
    gsum = (jnp.arange(MXU_W)[:, None] // DIFF_HEAD_DIM == jnp.arange(MXU_W)[None, :] // DIFF_HEAD_DIM).astype(BF16)
    out = jax.ShapeDtypeStruct((t, D_QK), BF16)
    return pl.pallas_call(
        _qkprep_body,
        grid=(t // tm,),
        in_specs=[pl.BlockSpec((tm, D_QK), lambda i: (i, 2)),
                  pl.BlockSpec((tm, D_QK), lambda i: (i, 3)),
                  pl.BlockSpec((tm, D_V), lambda i: (i, 4)),
                  pl.BlockSpec((tm, LANES), lambda i: (i, 0)),
                  pl.BlockSpec((tm, LANES), lambda i: (i, 0)),
                  pl.BlockSpec((1, D_QK), lambda i: (0, 0)),
                  pl.BlockSpec((1, D_QK), lambda i: (0, 0)),
                  pl.BlockSpec((MXU_W, MXU_W), lambda i: (0, 0))],
        out_specs=[pl.BlockSpec((tm, D_QK), lambda i: (i, 0))] * 3,
        out_shape=[out, out, out],
        compiler_params=_cparams(1),
        name="qk_prep",
    )(proj, proj, proj, cos_t, sin_t, gq, gk, gsum)


KV_UNROLL = 4


HEADS_PER_STEP = 2


def _attn_body(lam_ref, q_ref, k_ref, v_ref, sg_ref, o_ref, qs_ref, vx_ref, m_ref, acc_ref, *, tq, row_chunk,
               out_scale):
    i = pl.program_id(2)
    heads = range(HEADS_PER_STEP)

    @pl.when(i == 0)
    def _():
        for hh in heads:
            vx_ref[hh, :, 0:LANES] = v_ref[:, hh * LANES:(hh + 1) * LANES]
            vx_ref[hh, :, LANES:2 * LANES] = jnp.ones((vx_ref.shape[1], LANES), BF16)

    lane = lax.broadcasted_iota(I32, (1, LANES), 1)
    for hh in heads:
        q = q_ref[:, hh * LANES:(hh + 1) * LANES]
        zero = jnp.zeros_like(q)
        qs_ref[hh, 0:tq, :] = jnp.where(lane < DIFF_HEAD_DIM, q, zero)
        qs_ref[hh, tq:2 * tq, :] = jnp.where(lane >= DIFF_HEAD_DIM, q, zero)
    m_ref[...] = jnp.full_like(m_ref, -jnp.inf)
    acc_ref[...] = jnp.zeros_like(acc_ref)

    def step(j, masked):
        start = pl.multiple_of(j * tq, tq)
        for rc in range(2 * tq // row_chunk):
            rows = slice(rc * row_chunk, (rc + 1) * row_chunk)
            for hh in heads:
                k = k_ref[pl.ds(start, tq), hh * LANES:(hh + 1) * LANES]
                s = lax.dot_general(qs_ref[hh, rows, :], k, (((1,), (1,)), ((), ())),
                                    preferred_element_type=F32)
                if masked:
                    r = lax.broadcasted_iota(I32, (row_chunk, tq), 0) + (rc * row_chunk) % tq
                    c = lax.broadcasted_iota(I32, (row_chunk, tq), 1)
                    s = jnp.where(c <= r, s, -jnp.inf)
                m_prev = m_ref[hh, rows, :]
                m_new = jnp.maximum(m_prev, jnp.max(s, axis=1, keepdims=True))
                alpha = jnp.exp2(m_prev - m_new)
                p = jnp.exp2(s - jnp.tile(m_new, (1, tq // LANES)))
                pv = jnp.dot(p.astype(BF16), vx_ref[hh, pl.ds(start, tq), :], preferred_element_type=F32)
                acc_ref[hh, rows, :] = jnp.tile(alpha, (1, 2)) * acc_ref[hh, rows, :] + pv
                m_ref[hh, rows, :] = m_new

    def off_diagonal_group(jj, carry):
        for u in range(KV_UNROLL):
            step(KV_UNROLL * jj + u, False)
        return carry

    lax.fori_loop(0, i // KV_UNROLL, off_diagonal_group, 0)
    done = (i // KV_UNROLL) * KV_UNROLL
    for rem in range(KV_UNROLL):
        @pl.when(i - done == rem)
        def _(rem=rem):
            for u in range(rem):
                step(done + u, False)
            step(i, True)

    for hh in heads:
        acc = acc_ref[hh]
        o = acc[:, 0:LANES] / acc[:, LANES:2 * LANES]
        d = o[0:tq, :] - lam_ref[0] * o[tq:2 * tq, :]
        ms = jnp.mean(d * d, axis=-1, keepdims=True)
        o_ref[:, hh * LANES:(hh + 1) * LANES] = (d * lax.rsqrt(ms + NORM_EPS) * sg_ref[...]
                                                 * out_scale).astype(BF16)


def _attention(qr, kr, vb, lam, subln_g, bsz, seq, out_scale):
    t = bsz * seq
    tq = 512
    nq = seq // tq
    hps = HEADS_PER_STEP
    gw = hps * LANES
    body = functools.partial(_attn_body, tq=tq, row_chunk=128, out_scale=out_scale)
    grid_spec = pltpu.PrefetchScalarGridSpec(
        num_scalar_prefetch=1,
        grid=(bsz, DIFF_HEADS // hps, nq),
        in_specs=[pl.BlockSpec((tq, gw), lambda b, h, i, lam: (b * nq + i, h)),
                  pl.BlockSpec((seq, gw), lambda b, h, i, lam: (b, h)),
                  pl.BlockSpec((seq, gw), lambda b, h, i, lam: (b, h)),
                  pl.BlockSpec((1, LANES), lambda b, h, i, lam: (0, 0))],
        out_specs=pl.BlockSpec((tq, gw), lambda b, h, i, lam: (b * nq + i, h)),
        scratch_shapes=[pltpu.VMEM((hps, 2 * tq, LANES), BF16), pltpu.VMEM((hps, seq, 2 * LANES), BF16),
                        pltpu.VMEM((hps, 2 * tq, LANES), F32), pltpu.VMEM((hps, 2 * tq, 2 * LANES), F32)],
    )
    return pl.pallas_call(
        body, grid_spec=grid_spec,
        out_shape=jax.ShapeDtypeStruct((t, D_ATTN), BF16),
        compiler_params=_cparams(3),
        name="diff_attention",
    )(lam, qr, kr, vb, subln_g[None, :])


def _outproj_body(x_ref, ys_ref, ya_ref, wt_ref, wb_ref, g2_ref, wr_ref, br_ref, ltri_ref,
                  h1_ref, hp_ref, ti_ref, tg_ref, cnt_out_ref, cnt_ref):
    h1 = (x_ref[...] + jnp.dot(ys_ref[...], wt_ref[...], preferred_element_type=F32)
          + jnp.dot(ya_ref[...], wb_ref[...], preferred_element_type=F32))
    h1_ref[...] = h1
    ms = jnp.mean(h1 * h1, axis=-1, keepdims=True)
    h2 = h1 * lax.rsqrt(ms + NORM_EPS) * g2_ref[...]
    hp_ref[...] = _pack2(h2[:, :HALF], h2[:, HALF:])

    h2_hi = h2.astype(BF16)
    h2_lo = (h2 - h2_hi.astype(F32)).astype(BF16)
    both = jnp.dot(h2_hi, wr_ref[...], preferred_element_type=F32)
    logits = (both[:, :LANES] + both[:, LANES:]
              + jnp.dot(h2_lo, wr_ref[:, :LANES], preferred_element_type=F32) + br_ref[...])
    lane = lax.broadcasted_iota(I32, (1, LANES), 1)
    lanef = lane.astype(F32)
    cur = jnp.where(lane < N_EXPERTS, logits, -jnp.inf)
    vals, idxs = [], []
    for _ in range(TOP_K):
        m = jnp.max(cur, axis=1, keepdims=True)
        idx = jnp.min(jnp.where(cur == m, lanef, float(LANES)), axis=1, keepdims=True)
        vals.append(m)
        idxs.append(idx)
        cur = jnp.where(lanef == idx, -jnp.inf, cur)
    es = [jnp.exp(v - vals[0]) for v in vals]
    den = es[0] + es[1] + es[2] + es[3]

    @pl.when(pl.program_id(0) == 0)
    def _():
        cnt_ref[...] = jnp.zeros_like(cnt_ref)

    base = cnt_ref[...]
    ltri = ltri_ref[...]
    ti = jnp.zeros(logits.shape, F32)
    tg = jnp.zeros(logits.shape, F32)
    for kk in range(TOP_K):
        oh = lanef == idxs[kk]
        ohf = jnp.where(oh, 1.0, 0.0)
        pre = jnp.dot(ltri, ohf.astype(BF16), preferred_element_type=F32)
        rank = jnp.sum(jnp.where(oh, pre + base, 0.0), axis=1, keepdims=True)
        base = base + jnp.sum(ohf, axis=0, keepdims=True)
        ti = jnp.where(lane == kk, idxs[kk], ti)
        ti = jnp.where(lane == TOP_K + kk, rank, ti)
        tg = jnp.where(lane == kk, es[kk] / den, tg)
    cnt_ref[...] = base
    ti_ref[...] = ti.astype(I32)
    tg_ref[...] = tg
    cnt_out_ref[...] = jnp.broadcast_to(base, cnt_out_ref.shape).astype(I32)


def _out_proj_router(x2, y_ssm, y_attn, w_o, g2, w_r, b_r):
    t, d = x2.shape
    tm = 512
    w_rh = w_r.astype(BF16)
    w_rhl = jnp.concatenate([w_rh, (w_r - w_rh.astype(F32)).astype(BF16)], axis=1)
    ltri = (jnp.arange(tm)[:, None] > jnp.arange(tm)[None, :]).astype(BF16)

    def full(shape):
        return pl.BlockSpec(shape, lambda i: (0, 0))

    return pl.pallas_call(
        _outproj_body,
        grid=(t // tm,),
        in_specs=[pl.BlockSpec((tm, d), lambda i: (i, 0)),
                  pl.BlockSpec((tm, D_SSM), lambda i: (i, 0)),
                  pl.BlockSpec((tm, D_ATTN), lambda i: (i, 0)),
                  pl.BlockSpec((D_SSM, d), lambda i: (0, 0)), pl.BlockSpec((D_ATTN, d), lambda i: (1, 0)),
                  full((1, d)), full((d, 2 * LANES)), full((1, LANES)), full((tm, tm))],
        out_specs=[pl.BlockSpec((tm, d), lambda i: (i, 0)),
                   pl.BlockSpec((tm, HALF), lambda i: (i, 0)),
                   pl.BlockSpec((tm, LANES), lambda i: (i, 0)),
                   pl.BlockSpec((tm, LANES), lambda i: (i, 0)),
                   full((SUBLANES, LANES))],
        out_shape=[jax.ShapeDtypeStruct((t, d), F32), jax.ShapeDtypeStruct((t, HALF), U32),
                   jax.ShapeDtypeStruct((t, LANES), I32), jax.ShapeDtypeStruct((t, LANES), F32),
                   jax.ShapeDtypeStruct((SUBLANES, LANES), I32)],
        scratch_shapes=[pltpu.VMEM((1, LANES), F32)],
        compiler_params=_cparams(1),
        name="out_proj_router",
    )(x2, y_ssm, y_attn, w_o, w_o, g2, w_rhl, b_r, ltri)


def _row_copy(src_hbm, src_row, dst_ref, dst_row, sem):
    return pltpu.make_async_copy(src_hbm.at[pl.ds(src_row, 1), :], dst_ref.at[pl.ds(dst_row, 1), :], sem)


DMA_UNROLL = 8


def _zero_rows_body(o_ref):
    zero = jnp.zeros(o_ref.shape, F32)
    o_ref[...] = _pack2(zero, zero)


def _zero_rows(n_rows, rows_per_step):
    return pl.pallas_call(
        _zero_rows_body,
        grid=(n_rows // rows_per_step,),
        out_specs=pl.BlockSpec((rows_per_step, HALF), lambda i: (i, 0)),
        out_shape=jax.ShapeDtypeStruct((n_rows, HALF), U32),
        compiler_params=_cparams(1),
        name="zero_rows",
    )()


def _dispatch_body(pos_ref, x_ref, init_hbm, dst_hbm, sem, *, tokens):
    del init_hbm
    n = TOP_K * tokens

    def start(g, carry):
        for u in range(DMA_UNROLL):
            a = g * DMA_UNROLL + u
            t = g * (DMA_UNROLL // TOP_K) + u // TOP_K
            _row_copy(x_ref, t, dst_hbm, pos_ref[0, 0, a], sem).start(priority=u % 2)
        return carry

    def wait(g, carry):
        for u in range(DMA_UNROLL):
            t = g * (DMA_UNROLL // TOP_K) + u // TOP_K
            _row_copy(x_ref, t, dst_hbm, 0, sem).wait()
        return carry

    lax.fori_loop(0, n // DMA_UNROLL, start, 0)
    lax.fori_loop(0, n // DMA_UNROLL, wait, 0)


def _dispatch_rows(src, pos, n_rows):
    t, w = src.shape
    tokens = 512
    nsteps = t // tokens
    return pl.pallas_call(
        functools.partial(_dispatch_body, tokens=tokens),
        grid=(nsteps,),
        in_specs=[pl.BlockSpec((1, 1, TOP_K * tokens), lambda i: (i, 0, 0), memory_space=pltpu.SMEM),
                  pl.BlockSpec((tokens, w), lambda i: (i, 0)),
                  pl.BlockSpec(memory_space=pl.ANY)],
        out_specs=pl.BlockSpec(memory_space=pl.ANY),
        out_shape=jax.ShapeDtypeStruct((n_rows, w), src.dtype),
        scratch_shapes=[pltpu.SemaphoreType.DMA(())],
        input_output_aliases={2: 0},
        compiler_params=_cparams(1),
        name="dispatch_rows",
    )(pos.reshape(nsteps, 1, TOP_K * tokens), src, _zero_rows(n_rows, 2048))


MXU_W = 256


def _moe_up_body(te_ref, nv_ref, x_ref, w_ref, p_ref, bg_ref, bl_ref, o_ref, wg_ref, wl_ref):
    i = pl.program_id(1)
    new_expert = jnp.logical_or(i == 0, te_ref[i] != te_ref[jnp.maximum(i - 1, 0)])

    @pl.when(new_expert)
    def _():
        half = MXU_W // 2
        for c in range(w_ref.shape[1] // MXU_W):
            wc = w_ref[:, c * MXU_W:(c + 1) * MXU_W].astype(BF16)
            r = jnp.dot(wc, p_ref[...], preferred_element_type=F32)
            wg_ref[:, c * half:(c + 1) * half] = r[:, :half].astype(BF16)
            wl_ref[:, c * half:(c + 1) * half] = r[:, half:].astype(BF16)

    @pl.when(i < nv_ref[0])
    def _():
        w = x_ref[...]
        lo = _unpack2(w, 0).astype(BF16)
        hi = _unpack2(w, 1).astype(BF16)

        def up(ws_ref, b_ref):
            return (jnp.dot(lo, ws_ref[0:HALF, :], preferred_element_type=F32)
                    + jnp.dot(hi, ws_ref[HALF:, :], preferred_element_type=F32) + b_ref[...])

        glu = jnp.minimum(up(wg_ref, bg_ref), SWIGLU_LIMIT)
        lin = jnp.clip(up(wl_ref, bl_ref), -SWIGLU_LIMIT, SWIGLU_LIMIT)
        o_ref[...] = (glu * _sigmoid(SWIGLU_ALPHA * glu) * (lin + 1.0)).astype(BF16)

    @pl.when(i >= nv_ref[0])
    def _():
        o_ref[...] = jnp.zeros_like(o_ref)


def _moe_up(tile_expert, n_valid, xs, w1, b1g, b1l):
    r = xs.shape[0]
    tm, tn = MOE_TM, 1024
    nt = r // tm
    j = jnp.arange(MXU_W)
    perm = (j[None, :] == jnp.where(j % 2 == 0, j // 2, MXU_W // 2 + j // 2)[:, None]).astype(BF16)
    grid_spec = pltpu.PrefetchScalarGridSpec(
        num_scalar_prefetch=2,
        grid=(D_EXPERT // tn, nt),
        in_specs=[pl.BlockSpec((tm, HALF), lambda n, i, te, nv: (i, 0)),
                  pl.BlockSpec((None, D_MODEL, 2 * tn), lambda n, i, te, nv: (te[i], 0, n)),
                  pl.BlockSpec((MXU_W, MXU_W), lambda n, i, te, nv: (0, 0)),
                  pl.BlockSpec((None, 1, tn), lambda n, i, te, nv: (te[i], 0, n)),
                  pl.BlockSpec((None, 1, tn), lambda n, i, te, nv: (te[i], 0, n))],
        out_specs=pl.BlockSpec((tm, tn), lambda n, i, te, nv: (i, n)),
        scratch_shapes=[pltpu.VMEM((D_MODEL, tn), BF16), pltpu.VMEM((D_MODEL, tn), BF16)],
    )
    return pl.pallas_call(
        _moe_up_body, grid_spec=grid_spec,
        out_shape=jax.ShapeDtypeStruct((r, D_EXPERT), BF16),
        compiler_params=_cparams(2),
        name="moe_up",
    )(tile_expert, n_valid, xs, w1, perm, b1g, b1l)


def _moe_down_body(te_ref, nv_ref, a_ref, w_ref, b_ref, o_ref, wb_ref):
    i = pl.program_id(0)
    new_expert = jnp.logical_or(i == 0, te_ref[i] != te_ref[jnp.maximum(i - 1, 0)])

    @pl.when(new_expert)
    def _():
        wb_ref[...] = w_ref[...].astype(BF16)

    @pl.when(i < nv_ref[0])
    def _():
        y = jnp.dot(a_ref[...], wb_ref[...], preferred_element_type=F32) + b_ref[...]
        o_ref[...] = _pack2(y[:, :HALF], y[:, HALF:])

    @pl.when(i >= nv_ref[0])
    def _():
        zero = jnp.zeros(o_ref.shape, F32)
        o_ref[...] = _pack2(zero, zero)


def _moe_down(tile_expert, n_valid, act, w2b, b2):
    r = act.shape[0]
    tm = MOE_TM
    nt = r // tm
    grid_spec = pltpu.PrefetchScalarGridSpec(
        num_scalar_prefetch=2,
        grid=(nt,),
        in_specs=[pl.BlockSpec((tm, D_EXPERT), lambda i, te, nv: (i, 0)),
                  pl.BlockSpec((None, D_EXPERT, D_MODEL), lambda i, te, nv: (te[i], 0, 0)),
                  pl.BlockSpec((None, 1, D_MODEL), lambda i, te, nv: (te[i], 0, 0))],
        out_specs=pl.BlockSpec((tm, HALF), lambda i, te, nv: (i, 0)),
        scratch_shapes=[pltpu.VMEM((D_EXPERT, D_MODEL), BF16)],
    )
    return pl.pallas_call(
        _moe_down_body, grid_spec=grid_spec,
        out_shape=jax.ShapeDtypeStruct((r, HALF), U32),
        compiler_params=_cparams(1),
        name="moe_down",
    )(tile_expert, n_valid, act, w2b, b2)


def _combine_body(idx_ref, idx_next_ref, y_hbm, h1_ref, g_ref, o_ref, buf_ref, sem, *, tc):
    n = TOP_K * tc
    i = pl.program_id(0)
    slot = i % 2

    def issue(ids_ref, s):
        def start(g, carry):
            for u in range(DMA_UNROLL):
                a = g * DMA_UNROLL + u
                _row_copy(y_hbm, ids_ref[0, 0, a], buf_ref.at[s], a, sem.at[s]).start(priority=u % 2)
            return carry
        lax.fori_loop(0, n // DMA_UNROLL, start, 0)

    @pl.when(i == 0)
    def _():
        issue(idx_ref, 0)

    @pl.when(i + 1 < pl.num_programs(0))
    def _():
        issue(idx_next_ref, 1 - slot)

    def wait(g, carry):
        for u in range(DMA_UNROLL):
            a = g * DMA_UNROLL + u
            _row_copy(y_hbm, 0, buf_ref.at[slot], a, sem.at[slot]).wait()
        return carry

    lax.fori_loop(0, n // DMA_UNROLL, wait, 0)

    g = g_ref[...]
    lo = h1_ref[:, :HALF]
    hi = h1_ref[:, HALF:]
    for kk in range(TOP_K):
        w = buf_ref[slot, kk * tc:(kk + 1) * tc, :]
        gk = g[:, kk:kk + 1]
        lo = lo + gk * _unpack2(w, 0)
        hi = hi + gk * _unpack2(w, 1)
    o_ref[:, :HALF] = lo
    o_ref[:, HALF:] = hi


def _combine(pos_km, y_rows, h1, gates):
    t, d = h1.shape
    tc = COMBINE_TC
    nsteps = t // tc
    return pl.pallas_call(
        functools.partial(_combine_body, tc=tc),
        grid=(nsteps,),
        in_specs=[pl.BlockSpec((1, 1, TOP_K * tc), lambda i: (i, 0, 0), memory_space=pltpu.SMEM),
                  pl.BlockSpec((1, 1, TOP_K * tc), lambda i: (jnp.minimum(i + 1, nsteps - 1), 0, 0),
                               memory_space=pltpu.SMEM),
                  pl.BlockSpec(memory_space=pl.ANY),
                  pl.BlockSpec((tc, d), lambda i: (i, 0)),
                  pl.BlockSpec((tc, LANES), lambda i: (i, 0))],
        out_specs=pl.BlockSpec((tc, d), lambda i: (i, 0)),
        out_shape=jax.ShapeDtypeStruct((t, d), F32),
        scratch_shapes=[pltpu.VMEM((2, TOP_K * tc, HALF), U32), pltpu.SemaphoreType.DMA((2,))],
        compiler_params=_cparams(1),
        name="moe_combine",
    )(pos_km, pos_km, y_rows, h1, gates)


def _route(ti, counts, tm):
    t = ti.shape[0]
    n_assign = t * TOP_K
    n_rows = n_assign + N_EXPERTS * tm
    nt = n_rows // tm
    top_e = ti[:, :TOP_K]
    rank = ti[:, TOP_K:2 * TOP_K]
    counts = counts[0, :N_EXPERTS]
    tiles_per = (counts + (tm - 1)) // tm
    tile_end = jnp.cumsum(tiles_per)
    pstart = (tile_end - tiles_per) * tm
    experts = jnp.arange(N_EXPERTS, dtype=I32)
    onehot = top_e[:, :, None] == experts[None, None, :]
    pos = jnp.sum(jnp.where(onehot, pstart[None, None, :], 0), axis=-1) + rank
    n_valid = tile_end[N_EXPERTS - 1]
    tile_ids = jnp.minimum(jnp.arange(nt, dtype=I32), n_valid - 1)
    tile_expert = jnp.sum((tile_end[None, :] <= tile_ids[:, None]).astype(I32), axis=1)
    tile_expert = jnp.minimum(tile_expert, N_EXPERTS - 1)
    return pos, tile_expert, n_valid.reshape(1), n_rows


def _layer(h, cos_t, sin_t, lam_init, norm1_g, w_in, conv_w, conv_b, dt_bias, a_log, d_skip, ssm_norm_g,
           q_norm_g, k_norm_g, lq1, lk1, lq2, lk2, subln_g, w_out, norm2_g, w_router, b_router, w1, b1, w2, b2):
    bsz, seq, d = h.shape
    t = bsz * seq
    x2 = h.reshape(t, d)

    o_z = 0
    o_xbc = D_SSM
    o_dt = o_xbc + D_SSM + 2 * BC_W
    o_q = o_dt + SSM_HEADS
    o_k = o_q + D_QK
    o_v = o_k + D_QK
    w_p = jnp.concatenate([
        w_in[:, o_z:o_z + D_SSM], w_in[:, o_xbc:o_xbc + D_SSM],
        w_in[:, o_q:o_q + D_QK], w_in[:, o_k:o_k + D_QK], w_in[:, o_v:o_v + D_V],
        w_in[:, o_xbc + D_SSM:o_xbc + D_SSM + 2 * BC_W],
        jnp.pad(w_in[:, o_dt:o_dt + SSM_HEADS], ((0, 0), (0, LANES - SSM_HEADS + PROJ_PAD)))],
        axis=1).astype(BF16)

    proj = _in_proj(x2, norm1_g[None, :], w_p)
    y_ssm = _ssd(proj, bsz, seq, conv_w, conv_b, dt_bias, a_log, d_skip, ssm_norm_g)

    reps = D_QK // DIFF_HEAD_DIM
    qr, kr, vb = _qk_prep(proj, cos_t, sin_t, jnp.tile(q_norm_g, reps)[None, :], jnp.tile(k_norm_g, reps)[None, :])
    lam = (jnp.exp(jnp.sum(lq1 * lk1)) - jnp.exp(jnp.sum(lq2 * lk2)) + lam_init).reshape(1).astype(F32)
    y_attn = _attention(qr, kr, vb, lam, subln_g, bsz, seq, 1.0 - lam_init)

    w_ob = w_out.astype(BF16)
    w_r = jnp.pad(w_router, ((0, 0), (0, LANES - N_EXPERTS)))
    b_r = jnp.pad(b_router, (0, LANES - N_EXPERTS))[None, :]
    h1, h2p, ti, tg, counts = _out_proj_router(x2, y_ssm, y_attn, w_ob, norm2_g[None, :], w_r, b_r)

    pos, tile_expert, n_valid, n_rows = _route(ti, counts, MOE_TM)
    xs = _dispatch_rows(h2p, pos, n_rows)
    act = _moe_up(tile_expert, n_valid, xs, w1, b1[:, None, 0::2], b1[:, None, 1::2])
    y_rows = _moe_down(tile_expert, n_valid, act, w2, b2[:, None, :])

    tc = COMBINE_TC
    pos_km = pos.reshape(t // tc, tc, TOP_K).transpose(0, 2, 1).reshape(t // tc, 1, TOP_K * tc)
    out = _combine(pos_km, y_rows, h1, tg)
    return out.reshape(bsz, seq, d)


def kernel(x, positions, norm1_g, w_in, conv_w, conv_b, dt_bias, a_log, d_skip, ssm_norm_g, q_norm_g, k_norm_g,
           lambda_q1, lambda_k1, lambda_q2, lambda_k2, subln_g, w_out, norm2_g, w_router, b_router, w1, b1, w2, b2):
    bsz, seq, _ = x.shape
    inv = ROPE_THETA ** (-jnp.arange(0, DIFF_HEAD_DIM, 2, dtype=F32) / DIFF_HEAD_DIM)
    ang = positions.astype(F32).reshape(bsz * seq, 1) * inv[None, :]
    cos, sin = jnp.cos(ang), jnp.sin(ang)
    cos_t = jnp.tile(cos, (1, 2 * LANES // DIFF_HEAD_DIM))
    sin_t = jnp.tile(jnp.concatenate([-sin, sin], axis=1), (1, LANES // DIFF_HEAD_DIM))

    h = x
    for layer in range(norm1_g.shape[0]):
        lam_init = 0.8 - 0.6 * math.exp(-0.3 * layer)
        h = _layer(h, cos_t, sin_t, lam_init, norm1_g[layer], w_in[layer], conv_w[layer], conv_b[layer],
                   dt_bias[layer], a_log[layer], d_skip[layer], ssm_norm_g[layer], q_norm_g[layer],
                   k_norm_g[layer], lambda_q1[layer], lambda_k1[layer], lambda_q2[layer], lambda_k2[layer],
                   subln_g[layer], w_out[layer], norm2_g[layer], w_router[layer], b_router[layer],
                   w1[layer], b1[layer], w2[layer], b2[layer])
    return h
```

```python
import functools
import math

import jax
import jax.numpy as jnp
from jax import lax
from jax.experimental import pallas as pl
from jax.experimental.pallas import tpu as pltpu

F32 = jnp.float32
BF16 = jnp.bfloat16
U32 = jnp.uint32
I32 = jnp.int32

D_MODEL = 2048
D_SSM = 1024
D_ATTN = 1024
SSM_HEAD_DIM = 64
SSM_HEADS = 16
SSM_GROUPS = 2
SSM_STATE = 128
CONV_WIDTH = 4
CHUNK = 128
DIFF_HEAD_DIM = 64
DIFF_V_DIM = 128
DIFF_HEADS = 8
D_QK = 1024
D_V = 1024
ROPE_THETA = 10000.0
N_EXPERTS = 32
TOP_K = 4
D_EXPERT = 2048
SWIGLU_ALPHA = 1.702
SWIGLU_LIMIT = 7.0
NORM_EPS = 1e-5
LANES = 128
SUBLANES = 8
HALF = D_MODEL // 2

BC_W = SSM_GROUPS * SSM_STATE
PROJ_COLS = D_SSM + D_SSM + D_QK + D_QK + D_V + 2 * BC_W + LANES
INPROJ_TN = 1536
PROJ_PAD = -PROJ_COLS % INPROJ_TN

MOE_TM = 512
COMBINE_TC = 256
VMEM_LIMIT = 56 * 1024 * 1024


def _cparams(n_axes, vmem=VMEM_LIMIT):
    return pltpu.CompilerParams(dimension_semantics=("arbitrary",) * n_axes, vmem_limit_bytes=vmem)


def _sigmoid(v):
    return 1.0 / (1.0 + jnp.exp(-v))


def _pack2(lo, hi):
    return pltpu.pack_elementwise([lo, hi], packed_dtype=BF16)


def _unpack2(w, index):
    return pltpu.unpack_elementwise(w, index=index, packed_dtype=BF16, unpacked_dtype=F32)


def _inproj_body(x_ref, g_ref, w_ref, o_ref, u_ref):
    @pl.when(pl.program_id(1) == 0)
    def _():
        x = x_ref[...]
        ms = jnp.mean(x * x, axis=-1, keepdims=True)
        u_ref[...] = (x * lax.rsqrt(ms + NORM_EPS) * g_ref[...]).astype(BF16)

    o_ref[...] = jnp.dot(u_ref[...], w_ref[...], preferred_element_type=F32)


def _in_proj(x2, g, w_p):
    t, d = x2.shape
    ncol = w_p.shape[1]
    tm, tn = 1024, INPROJ_TN
    return pl.pallas_call(
        _inproj_body,
        grid=(t // tm, ncol // tn),
        in_specs=[pl.BlockSpec((tm, d), lambda i, j: (i, 0)),
                  pl.BlockSpec((1, d), lambda i, j: (0, 0)),
                  pl.BlockSpec((d, tn), lambda i, j: (0, j))],
        out_specs=pl.BlockSpec((tm, tn), lambda i, j: (i, j)),
        out_shape=jax.ShapeDtypeStruct((t, ncol), F32),
        scratch_shapes=[pltpu.VMEM((tm, d), BF16)],
        compiler_params=_cparams(2),
        name="in_proj",
    )(x2, g, w_p)


def _ssd_body(z_ref, x_ref, b_ref, c_ref, dt_ref, cwx_ref, cwb_ref, cwc_ref, cbx_ref, cbb_ref, cbc_ref,
              dtb_ref, alog_ref, dsk_ref, ng_ref, e_ref, y_ref, tx_ref, tb_ref, tc_ref, st_ref):
    @pl.when(pl.program_id(1) == 0)
    def _():
        tx_ref[...] = jnp.zeros_like(tx_ref)
        tb_ref[...] = jnp.zeros_like(tb_ref)
        tc_ref[...] = jnp.zeros_like(tc_ref)
        st_ref[...] = jnp.zeros_like(st_ref)

    row8 = lax.broadcasted_iota(I32, (SUBLANES, 1), 0)

    def conv_silu(u_ref, t_ref, w_ref, bias_ref):
        u = u_ref[...]
        tail = t_ref[...]
        w = w_ref[...]
        acc = u * w[CONV_WIDTH - 1:CONV_WIDTH, :] + bias_ref[...]
        for s in range(1, CONV_WIDTH):
            ru = pltpu.roll(u, s, 0)
            rt = pltpu.roll(tail, s, 0)
            head = jnp.where(row8 < s, rt, ru[0:SUBLANES, :])
            sh = jnp.concatenate([head, ru[SUBLANES:, :]], axis=0)
            acc = acc + sh * w[CONV_WIDTH - 1 - s:CONV_WIDTH - s, :]
        t_ref[...] = u[CHUNK - SUBLANES:CHUNK, :]
        return acc * _sigmoid(acc)

    xs = conv_silu(x_ref, tx_ref, cwx_ref, cbx_ref)
    bm = conv_silu(b_ref, tb_ref, cwb_ref, cbb_ref)
    cm = conv_silu(c_ref, tc_ref, cwc_ref, cbc_ref)

    dtr = dt_ref[...] + dtb_ref[...]
    dt = jnp.maximum(dtr, 0.0) + jnp.log1p(jnp.exp(-jnp.abs(dtr)))
    da = dt * (-jnp.exp(alog_ref[...]))

    rowi = lax.broadcasted_iota(I32, (CHUNK, 1), 0)
    acs = da
    k = 1
    while k < CHUNK:
        acs = acs + jnp.where(rowi >= k, pltpu.roll(acs, k, 0), 0.0)
        k *= 2
    acs_t = acs.T

    both = jnp.concatenate([dt, acs], axis=0)
    ex = None
    for _ in range(3):
        part = both.astype(BF16)
        term = jnp.dot(part, e_ref[...], preferred_element_type=F32)
        ex = term if ex is None else ex + term
        both = both - part.astype(F32)
    dt_f = ex[:CHUNK]
    a_f = ex[CHUNK:]
    a_last = a_f[CHUNK - 1:CHUNK, :]
    expa = jnp.exp(a_f)
    dstate = jnp.exp(a_last - a_f)
    cdec = jnp.exp(a_last)

    xdt = xs * dt_f
    xdt_b = xdt.astype(BF16)
    xds_b = (xdt * dstate).astype(BF16)

    coli = lax.broadcasted_iota(I32, (1, CHUNK), 1)
    tril = rowi >= coli
    lane = lax.broadcasted_iota(I32, (1, LANES), 1)
    hpg = SSM_HEADS // SSM_GROUPS
    gw = hpg * SSM_HEAD_DIM
    y_groups = []
    for g in range(SSM_GROUPS):
        bm_g = bm[:, g * SSM_STATE:(g + 1) * SSM_STATE]
        cm_g = cm[:, g * SSM_STATE:(g + 1) * SSM_STATE]
        cm_b = cm_g.astype(BF16)
        cb = lax.dot_general(cm_b, bm_g.astype(BF16), (((1,), (1,)), ((), ())),
                             preferred_element_type=F32)
        parts = []
        for pp in range(hpg // 2):
            p = g * (hpg // 2) + pp
            xp = xdt_b[:, p * LANES:(p + 1) * LANES]
            ys = []
            for hh in range(2):
                h = 2 * p + hh
                seg = acs[:, h:h + 1] - acs_t[h:h + 1, :]
                lm = jnp.where(tril, jnp.exp(seg), 0.0)
                ys.append(jnp.dot((cb * lm).astype(BF16), xp, preferred_element_type=F32))
            parts.append(jnp.where(lane < SSM_HEAD_DIM, ys[0], ys[1]))
        y_diag = jnp.concatenate(parts, axis=1)
        st_g = st_ref[:, g * gw:(g + 1) * gw]
        y_off = jnp.dot(cm_b, st_g.astype(BF16), preferred_element_type=F32) * expa[:, g * gw:(g + 1) * gw]
        new = jnp.dot(bm_g.T.astype(BF16), xds_b[:, g * gw:(g + 1) * gw], preferred_element_type=F32)
        st_ref[:, g * gw:(g + 1) * gw] = st_g * cdec[:, g * gw:(g + 1) * gw] + new
        y_groups.append(y_diag + y_off)

    zz = z_ref[...]
    gate = zz * _sigmoid(zz)
    dsk = dsk_ref[...]
    ng = ng_ref[...]
    outs = []
    for g in range(SSM_GROUPS):
        sl = slice(g * gw, (g + 1) * gw)
        yg = (y_groups[g] + xs[:, sl] * dsk[:, sl]) * gate[:, sl]
        ms = jnp.mean(yg * yg, axis=-1, keepdims=True)
        outs.append(yg * lax.rsqrt(ms + NORM_EPS) * ng[:, sl])
    y_ref[...] = jnp.concatenate(outs, axis=1).astype(BF16)


def _ssd(proj, bsz, seq, cw, cb, dt_bias, a_log, d_skip, norm_g):
    t = bsz * seq
    nc = seq // CHUNK
    cwx, cwb, cwc = cw[:, :D_SSM], cw[:, D_SSM:D_SSM + BC_W], cw[:, D_SSM + BC_W:]
    cb = cb[None, :]
    cbx, cbb, cbc = cb[:, :D_SSM], cb[:, D_SSM:D_SSM + BC_W], cb[:, D_SSM + BC_W:]
    pad = LANES - SSM_HEADS
    dtb = jnp.pad(dt_bias, (0, pad))[None, :]
    alog = jnp.pad(a_log, (0, pad))[None, :]
    dsk = jnp.repeat(d_skip, SSM_HEAD_DIM)[None, :]
    expand = (jnp.arange(D_SSM)[None, :] // SSM_HEAD_DIM == jnp.arange(LANES)[:, None]).astype(BF16)

    def row(b, c):
        return b * nc + c

    def full(shape):
        return pl.BlockSpec(shape, lambda b, c: (0, 0))

    bc0 = (2 * D_SSM + 2 * D_QK + D_V) // BC_W
    dt0 = (PROJ_COLS - LANES) // LANES
    return pl.pallas_call(
        _ssd_body,
        grid=(bsz, nc),
        in_specs=[pl.BlockSpec((CHUNK, D_SSM), lambda b, c: (row(b, c), 0)),
                  pl.BlockSpec((CHUNK, D_SSM), lambda b, c: (row(b, c), 1)),
                  pl.BlockSpec((CHUNK, BC_W), lambda b, c: (row(b, c), bc0)),
                  pl.BlockSpec((CHUNK, BC_W), lambda b, c: (row(b, c), bc0 + 1)),
                  pl.BlockSpec((CHUNK, LANES), lambda b, c: (row(b, c), dt0)),
                  full((CONV_WIDTH, D_SSM)), full((CONV_WIDTH, BC_W)), full((CONV_WIDTH, BC_W)),
                  full((1, D_SSM)), full((1, BC_W)), full((1, BC_W)),
                  full((1, LANES)), full((1, LANES)), full((1, D_SSM)), full((1, D_SSM)),
                  full((LANES, D_SSM))],
        out_specs=pl.BlockSpec((CHUNK, D_SSM), lambda b, c: (row(b, c), 0)),
        out_shape=jax.ShapeDtypeStruct((t, D_SSM), BF16),
        scratch_shapes=[pltpu.VMEM((SUBLANES, D_SSM), F32), pltpu.VMEM((SUBLANES, BC_W), F32),
                        pltpu.VMEM((SUBLANES, BC_W), F32), pltpu.VMEM((SSM_STATE, D_SSM), F32)],
        compiler_params=_cparams(2),
        name="ssd",
    )(proj, proj, proj, proj, proj, cwx, cwb, cwc, cbx, cbb, cbc, dtb, alog, dsk, norm_g[None, :], expand)


def _qkprep_body(q_ref, k_ref, v_ref, cos_ref, sin_ref, gq_ref, gk_ref, gm_ref, qo_ref, ko_ref, vo_ref):
    reps = D_QK // LANES
    cos = jnp.tile(cos_ref[...], (1, reps))
    sin = jnp.tile(sin_ref[...], (1, reps))
    lane = lax.broadcasted_iota(I32, (1, D_QK), 1)
    first = (lane & (DIFF_HEAD_DIM - 1)) < DIFF_HEAD_DIM // 2
    gm = gm_ref[...]

    def prep(t, g, scale):
        sq = t * t
        hi = sq.astype(BF16)
        lo = (sq - hi.astype(F32)).astype(BF16)
        ss = jnp.concatenate(
            [jnp.dot(hi[:, c:c + MXU_W], gm, preferred_element_type=F32)
             + jnp.dot(lo[:, c:c + MXU_W], gm, preferred_element_type=F32) for c in range(0, D_QK, MXU_W)], axis=1)
        tn = t * lax.rsqrt(ss * (1.0 / DIFF_HEAD_DIM) + NORM_EPS) * g
        half = DIFF_HEAD_DIM // 2
        rot = jnp.where(first, pltpu.roll(tn, D_QK - half, 1), pltpu.roll(tn, half, 1))
        return (tn * cos + rot * sin) * scale

    qo_ref[...] = prep(q_ref[...], gq_ref[...], DIFF_HEAD_DIM ** -0.5 * math.log2(math.e)).astype(BF16)
    ko_ref[...] = prep(k_ref[...], gk_ref[...], 1.0).astype(BF16)
    vo_ref[...] = v_ref[...].astype(BF16)


def _qk_prep(proj, cos_t, sin_t, gq, gk):
    t = proj.shape[0]
    tm = 512
    gsum = (jnp.arange(MXU_W)[:, None] // DIFF_HEAD_DIM == jnp.arange(MXU_W)[None, :] // DIFF_HEAD_DIM).astype(BF16)
    out = jax.ShapeDtypeStruct((t, D_QK), BF16)
    return pl.pallas_call(
        _qkprep_body,
        grid=(t // tm,),
        in_specs=[pl.BlockSpec((tm, D_QK), lambda i: (i, 2)),
                  pl.BlockSpec((tm, D_QK), lambda i: (i, 3)),
                  pl.BlockSpec((tm, D_V), lambda i: (i, 4)),
                  pl.BlockSpec((tm, LANES), lambda i: (i, 0)),
                  pl.BlockSpec((tm, LANES), lambda i: (i, 0)),
                  pl.BlockSpec((1, D_QK), lambda i: (0, 0)),
                  pl.BlockSpec((1, D_QK), lambda i: (0, 0)),
                  pl.BlockSpec((MXU_W, MXU_W), lambda i: (0, 0))],
        out_specs=[pl.BlockSpec((tm, D_QK), lambda i: (i, 0))] * 3,
        out_shape=[out, out, out],
        compiler_params=_cparams(1),
        name="qk_prep",
    )(proj, proj, proj, cos_t, sin_t, gq, gk, gsum)


KV_UNROLL = 4


HEADS_PER_STEP = 2


def _attn_body(lam_ref, q_ref, k_ref, v_ref, sg_ref, o_ref, qs_ref, vx_ref, m_ref, acc_ref, *, tq, row_chunk,
               out_scale):
    i = pl.program_id(2)
    heads = range(HEADS_PER_STEP)

    @pl.when(i == 0)
    def _():
        for hh in heads:
            vx_ref[hh, :, 0:LANES] = v_ref[:, hh * LANES:(hh + 1) * LANES]
            vx_ref[hh, :, LANES:2 * LANES] = jnp.ones((vx_ref.shape[1], LANES), BF16)

    lane = lax.broadcasted_iota(I32, (1, LANES), 1)
    for hh in heads:
        q = q_ref[:, hh * LANES:(hh + 1) * LANES]
        zero = jnp.zeros_like(q)
        qs_ref[hh, 0:tq, :] = jnp.where(lane < DIFF_HEAD_DIM, q, zero)
        qs_ref[hh, tq:2 * tq, :] = jnp.where(lane >= DIFF_HEAD_DIM, q, zero)
    m_ref[...] = jnp.full_like(m_ref, -jnp.inf)
    acc_ref[...] = jnp.zeros_like(acc_ref)

    def step(j, masked):
        start = pl.multiple_of(j * tq, tq)
        for rc in range(2 * tq // row_chunk):
            rows = slice(rc * row_chunk, (rc + 1) * row_chunk)
            for hh in heads:
                k = k_ref[pl.ds(start, tq), hh * LANES:(hh + 1) * LANES]
                s = lax.dot_general(qs_ref[hh, rows, :], k, (((1,), (1,)), ((), ())),
                                    preferred_element_type=F32)
                if masked:
                    r = lax.broadcasted_iota(I32, (row_chunk, tq), 0) + (rc * row_chunk) % tq
                    c = lax.broadcasted_iota(I32, (row_chunk, tq), 1)
                    s = jnp.where(c <= r, s, -jnp.inf)
                m_prev = m_ref[hh, rows, :]
                m_new = jnp.maximum(m_prev, jnp.max(s, axis=1, keepdims=True))
                alpha = jnp.exp2(m_prev - m_new)
                p = jnp.exp2(s - jnp.tile(m_new, (1, tq // LANES)))
                pv = jnp.dot(p.astype(BF16), vx_ref[hh, pl.ds(start, tq), :], preferred_element_type=F32)
                acc_ref[hh, rows, :] = jnp.tile(alpha, (1, 2)) * acc_ref[hh, rows, :] + pv
                m_ref[hh, rows, :] = m_new

    def off_diagonal_group(jj, carry):
        for u in range(KV_UNROLL):
            step(KV_UNROLL * jj + u, False)
        return carry

    lax.fori_loop(0, i // KV_UNROLL, off_diagonal_group, 0)
    done = (i // KV_UNROLL) * KV_UNROLL
    for rem in range(KV_UNROLL):
        @pl.when(i - done == rem)
        def _(rem=rem):
            for u in range(rem):
                step(done + u, False)
            step(i, True)

    for hh in heads:
        acc = acc_ref[hh]
        o = acc[:, 0:LANES] / acc[:, LANES:2 * LANES]
        d = o[0:tq, :] - lam_ref[0] * o[tq:2 * tq, :]
        ms = jnp.mean(d * d, axis=-1, keepdims=True)
        o_ref[:, hh * LANES:(hh + 1) * LANES] = (d * lax.rsqrt(ms + NORM_EPS) * sg_ref[...]
                                                 * out_scale).astype(BF16)


def _attention(qr, kr, vb, lam, subln_g, bsz, seq, out_scale):
    t = bsz * seq
    tq = 512
    nq = seq // tq
    hps = HEADS_PER_STEP
    gw = hps * LANES
    body = functools.partial(_attn_body, tq=tq, row_chunk=128, out_scale=out_scale)
    grid_spec = pltpu.PrefetchScalarGridSpec(
        num_scalar_prefetch=1,
        grid=(bsz, DIFF_HEADS // hps, nq),
        in_specs=[pl.BlockSpec((tq, gw), lambda b, h, i, lam: (b * nq + i, h)),
                  pl.BlockSpec((seq, gw), lambda b, h, i, lam: (b, h)),
                  pl.BlockSpec((seq, gw), lambda b, h, i, lam: (b, h)),
                  pl.BlockSpec((1, LANES), lambda b, h, i, lam: (0, 0))],
        out_specs=pl.BlockSpec((tq, gw), lambda b, h, i, lam: (b * nq + i, h)),
        scratch_shapes=[pltpu.VMEM((hps, 2 * tq, LANES), BF16), pltpu.VMEM((hps, seq, 2 * LANES), BF16),
                        pltpu.VMEM((hps, 2 * tq, LANES), F32), pltpu.VMEM((hps, 2 * tq, 2 * LANES), F32)],
    )
    return pl.pallas_call(
        body, grid_spec=grid_spec,
        out_shape=jax.ShapeDtypeStruct((t, D_ATTN), BF16),
        compiler_params=_cparams(3),
        name="diff_attention",
    )(lam, qr, kr, vb, subln_g[None, :])


def _outproj_body(x_ref, ys_ref, ya_ref, wt_ref, wb_ref, g2_ref, wr_ref, br_ref, ltri_ref,
                  h1_ref, hp_ref, ti_ref, tg_ref, cnt_out_ref, cnt_ref):
    h1 = (x_ref[...] + jnp.dot(ys_ref[...], wt_ref[...], preferred_element_type=F32)
          + jnp.dot(ya_ref[...], wb_ref[...], preferred_element_type=F32))
    h1_ref[...] = h1
    ms = jnp.mean(h1 * h1, axis=-1, keepdims=True)
    h2 = h1 * lax.rsqrt(ms + NORM_EPS) * g2_ref[...]
    hp_ref[...] = _pack2(h2[:, :HALF], h2[:, HALF:])

    h2_hi = h2.astype(BF16)
    h2_lo = (h2 - h2_hi.astype(F32)).astype(BF16)
    both = jnp.dot(h2_hi, wr_ref[...], preferred_element_type=F32)
    logits = (both[:, :LANES] + both[:, LANES:]
              + jnp.dot(h2_lo, wr_ref[:, :LANES], preferred_element_type=F32) + br_ref[...])
    lane = lax.broadcasted_iota(I32, (1, LANES), 1)
    lanef = lane.astype(F32)
    cur = jnp.where(lane < N_EXPERTS, logits, -jnp.inf)
    vals, idxs = [], []
    for _ in range(TOP_K):
        m = jnp.max(cur, axis=1, keepdims=True)
        idx = jnp.min(jnp.where(cur == m, lanef, float(LANES)), axis=1, keepdims=True)
        vals.append(m)
        idxs.append(idx)
        cur = jnp.where(lanef == idx, -jnp.inf, cur)
    es = [jnp.exp(v - vals[0]) for v in vals]
    den = es[0] + es[1] + es[2] + es[3]

    @pl.when(pl.program_id(0) == 0)
    def _():
        cnt_ref[...] = jnp.zeros_like(cnt_ref)

    base = cnt_ref[...]
    ltri = ltri_ref[...]
    ti = jnp.zeros(logits.shape, F32)
    tg = jnp.zeros(logits.shape, F32)
    for kk in range(TOP_K):
        oh = lanef == idxs[kk]
        ohf = jnp.where(oh, 1.0, 0.0)
        pre = jnp.dot(ltri, ohf.astype(BF16), preferred_element_type=F32)
        rank = jnp.sum(jnp.where(oh, pre + base, 0.0), axis=1, keepdims=True)
        base = base + jnp.sum(ohf, axis=0, keepdims=True)
        ti = jnp.where(lane == kk, idxs[kk], ti)
        ti = jnp.where(lane == TOP_K + kk, rank, ti)
        tg = jnp.where(lane == kk, es[kk] / den, tg)
    cnt_ref[...] = base
    ti_ref[...] = ti.astype(I32)
    tg_ref[...] = tg
    cnt_out_ref[...] = jnp.broadcast_to(base, cnt_out_ref.shape).astype(I32)


def _out_proj_router(x2, y_ssm, y_attn, w_o, g2, w_r, b_r):
    t, d = x2.shape
    tm = 512
    w_rh = w_r.astype(BF16)
    w_rhl = jnp.concatenate([w_rh, (w_r - w_rh.astype(F32)).astype(BF16)], axis=1)
    ltri = (jnp.arange(tm)[:, None] > jnp.arange(tm)[None, :]).astype(BF16)

    def full(shape):
        return pl.BlockSpec(shape, lambda i: (0, 0))

    return pl.pallas_call(
        _outproj_body,
        grid=(t // tm,),
        in_specs=[pl.BlockSpec((tm, d), lambda i: (i, 0)),
                  pl.BlockSpec((tm, D_SSM), lambda i: (i, 0)),
                  pl.BlockSpec((tm, D_ATTN), lambda i: (i, 0)),
                  pl.BlockSpec((D_SSM, d), lambda i: (0, 0)), pl.BlockSpec((D_ATTN, d), lambda i: (1, 0)),
                  full((1, d)), full((d, 2 * LANES)), full((1, LANES)), full((tm, tm))],
        out_specs=[pl.BlockSpec((tm, d), lambda i: (i, 0)),
                   pl.BlockSpec((tm, HALF), lambda i: (i, 0)),
                   pl.BlockSpec((tm, LANES), lambda i: (i, 0)),
                   pl.BlockSpec((tm, LANES), lambda i: (i, 0)),
                   full((SUBLANES, LANES))],
        out_shape=[jax.ShapeDtypeStruct((t, d), F32), jax.ShapeDtypeStruct((t, HALF), U32),
                   jax.ShapeDtypeStruct((t, LANES), I32), jax.ShapeDtypeStruct((t, LANES), F32),
                   jax.ShapeDtypeStruct((SUBLANES, LANES), I32)],
        scratch_shapes=[pltpu.VMEM((1, LANES), F32)],
        compiler_params=_cparams(1),
        name="out_proj_router",
    )(x2, y_ssm, y_attn, w_o, w_o, g2, w_rhl, b_r, ltri)


def _row_copy(src_hbm, src_row, dst_ref, dst_row, sem):
    return pltpu.make_async_copy(src_hbm.at[pl.ds(src_row, 1), :], dst_ref.at[pl.ds(dst_row, 1), :], sem)


DMA_UNROLL = 8


def _zero_rows_body(o_ref):
    zero = jnp.zeros(o_ref.shape, F32)
    o_ref[...] = _pack2(zero, zero)


def _zero_rows(n_rows, rows_per_step):
    return pl.pallas_call(
        _zero_rows_body,
        grid=(n_rows // rows_per_step,),
        out_specs=pl.BlockSpec((rows_per_step, HALF), lambda i: (i, 0)),
        out_shape=jax.ShapeDtypeStruct((n_rows, HALF), U32),
        compiler_params=_cparams(1),
        name="zero_rows",
    )()


def _dispatch_body(pos_ref, x_ref, init_hbm, dst_hbm, sem, *, tokens):
    del init_hbm
    n = TOP_K * tokens

    def start(g, carry):
        for u in range(DMA_UNROLL):
            a = g * DMA_UNROLL + u
            t = g * (DMA_UNROLL // TOP_K) + u // TOP_K
            _row_copy(x_ref, t, dst_hbm, pos_ref[0, 0, a], sem).start(priority=u % 2)
        return carry

    def wait(g, carry):
        for u in range(DMA_UNROLL):
            t = g * (DMA_UNROLL // TOP_K) + u // TOP_K
            _row_copy(x_ref, t, dst_hbm, 0, sem).wait()
        return carry

    lax.fori_loop(0, n // DMA_UNROLL, start, 0)
    lax.fori_loop(0, n // DMA_UNROLL, wait, 0)


def _dispatch_rows(src, pos, n_rows):
    t, w = src.shape
    tokens = 512
    nsteps = t // tokens
    return pl.pallas_call(
        functools.partial(_dispatch_body, tokens=tokens),
        grid=(nsteps,),
        in_specs=[pl.BlockSpec((1, 1, TOP_K * tokens), lambda i: (i, 0, 0), memory_space=pltpu.SMEM),
                  pl.BlockSpec((tokens, w), lambda i: (i, 0)),
                  pl.BlockSpec(memory_space=pl.ANY)],
        out_specs=pl.BlockSpec(memory_space=pl.ANY),
        out_shape=jax.ShapeDtypeStruct((n_rows, w), src.dtype),
        scratch_shapes=[pltpu.SemaphoreType.DMA(())],
        input_output_aliases={2: 0},
        compiler_params=_cparams(1),
        name="dispatch_rows",
    )(pos.reshape(nsteps, 1, TOP_K * tokens), src, _zero_rows(n_rows, 2048))


MXU_W = 256


def _moe_up_body(te_ref, nv_ref, x_ref, w_ref, p_ref, bg_ref, bl_ref, o_ref, wg_ref, wl_ref):
    i = pl.program_id(1)
    new_expert = jnp.logical_or(i == 0, te_ref[i] != te_ref[jnp.maximum(i - 1, 0)])

    @pl.when(new_expert)
    def _():
        half = MXU_W // 2
        for c in range(w_ref.shape[1] // MXU_W):
            wc = w_ref[:, c * MXU_W:(c + 1) * MXU_W].astype(BF16)
            r = jnp.dot(wc, p_ref[...], preferred_element_type=F32)
            wg_ref[:, c * half:(c + 1) * half] = r[:, :half].astype(BF16)
            wl_ref[:, c * half:(c + 1) * half] = r[:, half:].astype(BF16)

    @pl.when(i < nv_ref[0])
    def _():
        w = x_ref[...]
        lo = _unpack2(w, 0).astype(BF16)
        hi = _unpack2(w, 1).astype(BF16)

        def up(ws_ref, b_ref):
            return (jnp.dot(lo, ws_ref[0:HALF, :], preferred_element_type=F32)
                    + jnp.dot(hi, ws_ref[HALF:, :], preferred_element_type=F32) + b_ref[...])

        glu = jnp.minimum(up(wg_ref, bg_ref), SWIGLU_LIMIT)
        lin = jnp.clip(up(wl_ref, bl_ref), -SWIGLU_LIMIT, SWIGLU_LIMIT)
        o_ref[...] = (glu * _sigmoid(SWIGLU_ALPHA * glu) * (lin + 1.0)).astype(BF16)

    @pl.when(i >= nv_ref[0])
    def _():
        o_ref[...] = jnp.zeros_like(o_ref)


def _moe_up(tile_expert, n_valid, xs, w1, b1g, b1l):
    r = xs.shape[0]
    tm, tn = MOE_TM, 1024
    nt = r // tm
    j = jnp.arange(MXU_W)
    perm = (j[None, :] == jnp.where(j % 2 == 0, j // 2, MXU_W // 2 + j // 2)[:, None]).astype(BF16)
    grid_spec = pltpu.PrefetchScalarGridSpec(
        num_scalar_prefetch=2,
        grid=(D_EXPERT // tn, nt),
        in_specs=[pl.BlockSpec((tm, HALF), lambda n, i, te, nv: (i, 0)),
                  pl.BlockSpec((None, D_MODEL, 2 * tn), lambda n, i, te, nv: (te[i], 0, n)),
                  pl.BlockSpec((MXU_W, MXU_W), lambda n, i, te, nv: (0, 0)),
                  pl.BlockSpec((None, 1, tn), lambda n, i, te, nv: (te[i], 0, n)),
                  pl.BlockSpec((None, 1, tn), lambda n, i, te, nv: (te[i], 0, n))],
        out_specs=pl.BlockSpec((tm, tn), lambda n, i, te, nv: (i, n)),
        scratch_shapes=[pltpu.VMEM((D_MODEL, tn), BF16), pltpu.VMEM((D_MODEL, tn), BF16)],
    )
    return pl.pallas_call(
        _moe_up_body, grid_spec=grid_spec,
        out_shape=jax.ShapeDtypeStruct((r, D_EXPERT), BF16),
        compiler_params=_cparams(2),
        name="moe_up",
    )(tile_expert, n_valid, xs, w1, perm, b1g, b1l)


def _moe_down_body(te_ref, nv_ref, nx_ref, a_ref, w_hbm, b_ref, o_ref, stage_ref, wb_ref, sem):
    i = pl.program_id(0)
    e = te_ref[i]
    new_expert = jnp.logical_or(i == 0, e != te_ref[jnp.maximum(i - 1, 0)])

    def fetch(expert):
        return pltpu.make_async_copy(w_hbm.at[expert], stage_ref, sem)

    @pl.when(i == 0)
    def _():
        fetch(e).start()

    @pl.when(new_expert)
    def _():
        fetch(e).wait()
        wb_ref[...] = stage_ref[...].astype(BF16)
        nxt = nx_ref[e]

        @pl.when(nxt < N_EXPERTS)
        def _():
            fetch(nxt).start()

    @pl.when(i < nv_ref[0])
    def _():
        y = jnp.dot(a_ref[...], wb_ref[...], preferred_element_type=F32) + b_ref[...]
        o_ref[...] = _pack2(y[:, :HALF], y[:, HALF:])

    @pl.when(i >= nv_ref[0])
    def _():
        zero = jnp.zeros(o_ref.shape, F32)
        o_ref[...] = _pack2(zero, zero)


def _moe_down(tile_expert, n_valid, next_expert, act, w2, b2):
    r = act.shape[0]
    tm = MOE_TM
    nt = r // tm
    grid_spec = pltpu.PrefetchScalarGridSpec(
        num_scalar_prefetch=3,
        grid=(nt,),
        in_specs=[pl.BlockSpec((tm, D_EXPERT), lambda i, te, nv, nx: (i, 0)),
                  pl.BlockSpec(memory_space=pl.ANY),
                  pl.BlockSpec((None, 1, D_MODEL), lambda i, te, nv, nx: (te[i], 0, 0))],
        out_specs=pl.BlockSpec((tm, HALF), lambda i, te, nv, nx: (i, 0)),
        scratch_shapes=[pltpu.VMEM((D_EXPERT, D_MODEL), F32), pltpu.VMEM((D_EXPERT, D_MODEL), BF16),
                        pltpu.SemaphoreType.DMA(())],
    )
    return pl.pallas_call(
        _moe_down_body, grid_spec=grid_spec,
        out_shape=jax.ShapeDtypeStruct((r, HALF), U32),
        compiler_params=_cparams(1),
        name="moe_down",
    )(tile_expert, n_valid, next_expert, act, w2, b2)


def _combine_body(idx_ref, idx_next_ref, y_hbm, h1_ref, g_ref, o_ref, buf_ref, sem, *, tc):
    n = TOP_K * tc
    i = pl.program_id(0)
    slot = i % 2

    def issue(ids_ref, s):
        def start(g, carry):
            for u in range(DMA_UNROLL):
                a = g * DMA_UNROLL + u
                _row_copy(y_hbm, ids_ref[0, 0, a], buf_ref.at[s], a, sem.at[s]).start(priority=u % 2)
            return carry
        lax.fori_loop(0, n // DMA_UNROLL, start, 0)

    @pl.when(i == 0)
    def _():
        issue(idx_ref, 0)

    @pl.when(i + 1 < pl.num_programs(0))
    def _():
        issue(idx_next_ref, 1 - slot)

    def wait(g, carry):
        for u in range(DMA_UNROLL):
            a = g * DMA_UNROLL + u
            _row_copy(y_hbm, 0, buf_ref.at[slot], a, sem.at[slot]).wait()
        return carry

    lax.fori_loop(0, n // DMA_UNROLL, wait, 0)

    g = g_ref[...]
    lo = h1_ref[:, :HALF]
    hi = h1_ref[:, HALF:]
    for kk in range(TOP_K):
        w = buf_ref[slot, kk * tc:(kk + 1) * tc, :]
        gk = g[:, kk:kk + 1]
        lo = lo + gk * _unpack2(w, 0)
        hi = hi + gk * _unpack2(w, 1)
    o_ref[:, :HALF] = lo
    o_ref[:, HALF:] = hi


def _combine(pos_km, y_rows, h1, gates):
    t, d = h1.shape
    tc = COMBINE_TC
    nsteps = t // tc
    return pl.pallas_call(
        functools.partial(_combine_body, tc=tc),
        grid=(nsteps,),
        in_specs=[pl.BlockSpec((1, 1, TOP_K * tc), lambda i: (i, 0, 0), memory_space=pltpu.SMEM),
                  pl.BlockSpec((1, 1, TOP_K * tc), lambda i: (jnp.minimum(i + 1, nsteps - 1), 0, 0),
                               memory_space=pltpu.SMEM),
                  pl.BlockSpec(memory_space=pl.ANY),
                  pl.BlockSpec((tc, d), lambda i: (i, 0)),
                  pl.BlockSpec((tc, LANES), lambda i: (i, 0))],
        out_specs=pl.BlockSpec((tc, d), lambda i: (i, 0)),
        out_shape=jax.ShapeDtypeStruct((t, d), F32),
        scratch_shapes=[pltpu.VMEM((2, TOP_K * tc, HALF), U32), pltpu.SemaphoreType.DMA((2,))],
        compiler_params=_cparams(1),
        name="moe_combine",
    )(pos_km, pos_km, y_rows, h1, gates)


def _route(ti, counts, tm):
    t = ti.shape[0]
    n_assign = t * TOP_K
    n_rows = n_assign + N_EXPERTS * tm
    nt = n_rows // tm
    top_e = ti[:, :TOP_K]
    rank = ti[:, TOP_K:2 * TOP_K]
    counts = counts[0, :N_EXPERTS]
    tiles_per = (counts + (tm - 1)) // tm
    tile_end = jnp.cumsum(tiles_per)
    pstart = (tile_end - tiles_per) * tm
    experts = jnp.arange(N_EXPERTS, dtype=I32)
    onehot = top_e[:, :, None] == experts[None, None, :]
    pos = jnp.sum(jnp.where(onehot, pstart[None, None, :], 0), axis=-1) + rank
    n_valid = tile_end[N_EXPERTS - 1]
    tile_ids = jnp.minimum(jnp.arange(nt, dtype=I32), n_valid - 1)
    tile_expert = jnp.sum((tile_end[None, :] <= tile_ids[:, None]).astype(I32), axis=1)
    tile_expert = jnp.minimum(tile_expert, N_EXPERTS - 1)
    later = jnp.logical_and(experts[None, :] > experts[:, None], tiles_per[None, :] > 0)
    next_expert = jnp.min(jnp.where(later, experts[None, :], N_EXPERTS), axis=1).astype(I32)
    return pos, tile_expert, n_valid.reshape(1), next_expert, n_rows


def _layer(h, cos_t, sin_t, lam_init, norm1_g, w_in, conv_w, conv_b, dt_bias, a_log, d_skip, ssm_norm_g,
           q_norm_g, k_norm_g, lq1, lk1, lq2, lk2, subln_g, w_out, norm2_g, w_router, b_router, w1, b1, w2, b2):
    bsz, seq, d = h.shape
    t = bsz * seq
    x2 = h.reshape(t, d)

    o_z = 0
    o_xbc = D_SSM
    o_dt = o_xbc + D_SSM + 2 * BC_W
    o_q = o_dt + SSM_HEADS
    o_k = o_q + D_QK
    o_v = o_k + D_QK
    w_p = jnp.concatenate([
        w_in[:, o_z:o_z + D_SSM], w_in[:, o_xbc:o_xbc + D_SSM],
        w_in[:, o_q:o_q + D_QK], w_in[:, o_k:o_k + D_QK], w_in[:, o_v:o_v + D_V],
        w_in[:, o_xbc + D_SSM:o_xbc + D_SSM + 2 * BC_W],
        jnp.pad(w_in[:, o_dt:o_dt + SSM_HEADS], ((0, 0), (0, LANES - SSM_HEADS + PROJ_PAD)))],
        axis=1).astype(BF16)

    proj = _in_proj(x2, norm1_g[None, :], w_p)
    y_ssm = _ssd(proj, bsz, seq, conv_w, conv_b, dt_bias, a_log, d_skip, ssm_norm_g)

    reps = D_QK // DIFF_HEAD_DIM
    qr, kr, vb = _qk_prep(proj, cos_t, sin_t, jnp.tile(q_norm_g, reps)[None, :], jnp.tile(k_norm_g, reps)[None, :])
    lam = (jnp.exp(jnp.sum(lq1 * lk1)) - jnp.exp(jnp.sum(lq2 * lk2)) + lam_init).reshape(1).astype(F32)
    y_attn = _attention(qr, kr, vb, lam, subln_g, bsz, seq, 1.0 - lam_init)

    w_ob = w_out.astype(BF16)
    w_r = jnp.pad(w_router, ((0, 0), (0, LANES - N_EXPERTS)))
    b_r = jnp.pad(b_router, (0, LANES - N_EXPERTS))[None, :]
    h1, h2p, ti, tg, counts = _out_proj_router(x2, y_ssm, y_attn, w_ob, norm2_g[None, :], w_r, b_r)

    pos, tile_expert, n_valid, next_expert, n_rows = _route(ti, counts, MOE_TM)
    xs = _dispatch_rows(h2p, pos, n_rows)
    act = _moe_up(tile_expert, n_valid, xs, w1, b1[:, None, 0::2], b1[:, None, 1::2])
    y_rows = _moe_down(tile_expert, n_valid, next_expert, act, w2, b2[:, None, :])

    tc = COMBINE_TC
    pos_km = pos.reshape(t // tc, tc, TOP_K).transpose(0, 2, 1).reshape(t // tc, 1, TOP_K * tc)
    out = _combine(pos_km, y_rows, h1, tg)
    return out.reshape(bsz, seq, d)


def kernel(x, positions, norm1_g, w_in, conv_w, conv_b, dt_bias, a_log, d_skip, ssm_norm_g, q_norm_g, k_norm_g,
           lambda_q1, lambda_k1, lambda_q2, lambda_k2, subln_g, w_out, norm2_g, w_router, b_router, w1, b1, w2, b2):
    bsz, seq, _ = x.shape
    inv = ROPE_THETA ** (-jnp.arange(0, DIFF_HEAD_DIM, 2, dtype=F32) / DIFF_HEAD_DIM)
    ang = positions.astype(F32).reshape(bsz * seq, 1) * inv[None, :]
    cos, sin = jnp.cos(ang), jnp.sin(ang)
    cos_t = jnp.tile(cos, (1, 2 * LANES // DIFF_HEAD_DIM))
    sin_t = jnp.tile(jnp.concatenate([-sin, sin], axis=1), (1, LANES // DIFF_HEAD_DIM))

    h = x
    for layer in range(norm1_g.shape[0]):
        lam_init = 0.8 - 0.6 * math.exp(-0.3 * layer)
        h = _layer(h, cos_t, sin_t, lam_init, norm1_g[layer], w_in[layer], conv_w[layer], conv_b[layer],
                   dt_bias[layer], a_log[layer], d_skip[layer], ssm_norm_g[layer], q_norm_g[layer],
                   k_norm_g[layer], lambda_q1[layer], lambda_k1[layer], lambda_q2[layer], lambda_k2[layer],
                   subln_g[layer], w_out[layer], norm2_g[layer], w_router[layer], b_router[layer],
                   w1[layer], b1[layer], w2[layer], b2[layer])
    return h
```
